```python
import jax, jax.numpy as jnp
from jax import lax
import numpy as np

D_MODEL = 2048
BATCH = 16
SEQ = 2048
DEPTH = 2

CHUNK = 64
CONV_WIDTH = 1024
CONV_K = 3
HG_HEADS = 8
HG_DK = 128
HG_DV = 128
HG_WIDTH = HG_HEADS * HG_DK
HG_CHUNK = 16
ATT_HEADS = 8
ATT_DH = 128
ATT_WIDTH = ATT_HEADS * ATT_DH
ATT_LEFT_CHUNKS = 8
BAND = ATT_LEFT_CHUNKS + 1
MAX_REL_DIST = 256
D_FF = 4 * D_MODEL
PLE_DIM = 256
N_BRANCH = 3
EPS = 1e-6
LB_FLOOR = 1e-30
MASK_VALUE = -1e30
IN_COLS = 3 * CONV_WIDTH + 4 * HG_WIDTH + 3 * ATT_WIDTH + N_BRANCH * D_MODEL

kernel_name = "hybrid_gated_conv_hgrn2_chunkattn_encoder"


def rms_norm(x, g):
    x32 = x.astype(jnp.float32)
    y = x32 * lax.rsqrt(jnp.mean(x32 * x32, axis=-1, keepdims=True) + EPS)
    return (y * g.astype(jnp.float32)).astype(x.dtype)


def causal_depthwise_conv(u, w):
    return lax.conv_general_dilated(
        u, w[:, None, :].astype(u.dtype), window_strides=(1,),
        padding=[(CONV_K - 1, 0)], dimension_numbers=("NWC", "WIO", "NWC"),
        feature_group_count=u.shape[-1])


def hgrn2(q, f_logit, i, g, lb, norm_g):
    B, S, _ = q.shape
    n = S // HG_CHUNK
    f32 = jnp.float32
    shp = (B, n, HG_CHUNK, HG_HEADS, HG_DK)
    z = f_logit.astype(f32)
    lb = lb.astype(f32)
    log_f = jnp.logaddexp(jnp.log(jnp.maximum(lb, LB_FLOOR)),
                          jnp.log1p(-lb) + jax.nn.log_sigmoid(z))
    k = (1.0 - lb) * jax.nn.sigmoid(-z)
    qc = q.astype(f32).reshape(shp)
    kc = k.reshape(shp)
    vc = i.astype(f32).reshape(B, n, HG_CHUNK, HG_HEADS, HG_DV)
    G = jnp.cumsum(log_f.reshape(shp), axis=2)
    causal = jnp.tril(jnp.ones((HG_CHUNK, HG_CHUNK), bool))[None, None, :, :, None, None]
    diff = G[:, :, :, None] - G[:, :, None, :]
    decay = jnp.where(causal, jnp.exp(jnp.where(causal, diff, 0.0)), 0.0)
    A = jnp.einsum("bnthk,bnshk,bntshk->bnhts", qc, kc, decay)
    o_intra = jnp.einsum("bnhts,bnshv->bnthv", A, vc)
    G_last = G[:, :, -1]
    q_dec = qc * jnp.exp(G)
    k_dec = kc * jnp.exp(G_last[:, :, None] - G)
    chunk_decay = jnp.exp(G_last)

    def step(state, xs):
        qd, kd, v, dec = xs
        o = jnp.einsum("bthk,bhkv->bthv", qd, state)
        state = dec[..., None] * state + jnp.einsum("bshk,bshv->bhkv", kd, v)
        return state, o

    xs = (jnp.moveaxis(q_dec, 1, 0), jnp.moveaxis(k_dec, 1, 0),
          jnp.moveaxis(vc, 1, 0), jnp.moveaxis(chunk_decay, 1, 0))
    s0 = jnp.zeros((B, HG_HEADS, HG_DK, HG_DV), f32)
    _, o_inter = lax.scan(step, s0, xs)
    o = (o_intra + jnp.moveaxis(o_inter, 0, 1)).reshape(B, S, HG_HEADS, HG_DV)
    o = o * lax.rsqrt(jnp.mean(o * o, axis=-1, keepdims=True) + EPS) * norm_g.astype(f32)
    o = o.reshape(B, S, HG_WIDTH) * jax.nn.silu(g.astype(f32))
    return o.astype(q.dtype)


def rel_bias_index():
    a = np.arange(CHUNK)[:, None, None]
    j = np.arange(BAND)[None, :, None]
    b = np.arange(CHUNK)[None, None, :]
    rel = (BAND - 1 - j) * CHUNK + a - b
    idx = np.clip(rel, -MAX_REL_DIST, MAX_REL_DIST) + MAX_REL_DIST
    return idx.reshape(CHUNK, BAND * CHUNK)


def chunk_band_attention(q, k, v, rel_table):
    B, S, _ = q.shape
    n = S // CHUNK
    f32 = jnp.float32
    qc = q.reshape(B, n, CHUNK, ATT_HEADS, ATT_DH)
    pad = ((0, 0), (ATT_LEFT_CHUNKS, 0), (0, 0), (0, 0), (0, 0))
    kp = jnp.pad(k.reshape(B, n, CHUNK, ATT_HEADS, ATT_DH), pad)
    vp = jnp.pad(v.reshape(B, n, CHUNK, ATT_HEADS, ATT_DH), pad)
    band_idx = jnp.arange(n)[:, None] + jnp.arange(BAND)[None, :]
    kb = kp[:, band_idx].reshape(B, n, BAND * CHUNK, ATT_HEADS, ATT_DH)
    vb = vp[:, band_idx].reshape(B, n, BAND * CHUNK, ATT_HEADS, ATT_DH)
    valid = jnp.repeat(band_idx >= ATT_LEFT_CHUNKS, CHUNK, axis=1)
    bias = rel_table[:, rel_bias_index()].astype(f32)
    s = jnp.einsum("bnqhd,bnkhd->bnhqk", qc, kb).astype(f32) * (ATT_DH ** -0.5) + bias[None, None]
    s = jnp.where(valid[None, :, None, None, :], s, MASK_VALUE)
    pr = jax.nn.softmax(s, axis=-1).astype(v.dtype)
    o = jnp.einsum("bnhqk,bnkhd->bnqhd", pr, vb)
    return o.reshape(B, S, ATT_WIDTH)


def setup_inputs(seed: int = 0) -> dict:
    key = jax.random.key(seed)
    ks = jax.random.split(key, 20)
    f32 = jnp.float32

    def nrm(k, shape, scale):
        return jax.random.normal(k, shape, f32) * scale

    return {
        "x": nrm(ks[0], (BATCH, SEQ, D_MODEL), 1.0),
        "p": nrm(ks[1], (DEPTH, BATCH, SEQ, PLE_DIM), 1.0),
        "w_in": nrm(ks[2], (DEPTH, D_MODEL, IN_COLS), D_MODEL ** -0.5),
        "conv_w": nrm(ks[3], (DEPTH, CONV_K, CONV_WIDTH), CONV_K ** -0.5),
        "hg_lb_logits": nrm(ks[4], (DEPTH, HG_WIDTH), 0.5),
        "hg_norm_g": 1.0 + nrm(ks[5], (DEPTH, HG_DV), 0.02),
        "att_rel_bias": nrm(ks[6], (DEPTH, ATT_HEADS, 2 * MAX_REL_DIST + 1), 0.2),
        "w_branch": nrm(ks[7], (DEPTH, N_BRANCH, CONV_WIDTH, D_MODEL), CONV_WIDTH ** -0.5),
        "w_o": nrm(ks[8], (DEPTH, D_MODEL, D_MODEL), D_MODEL ** -0.5),
        "w_ff1": nrm(ks[9], (DEPTH, D_MODEL, D_FF), D_MODEL ** -0.5),
        "w_ff2": nrm(ks[10], (DEPTH, D_FF, D_MODEL), D_FF ** -0.5),
        "w_ple_in": nrm(ks[11], (DEPTH, PLE_DIM, D_MODEL), PLE_DIM ** -0.5),
        "w_ple_gate": nrm(ks[12], (DEPTH, D_MODEL, D_MODEL), D_MODEL ** -0.5),
        "g_mix": 1.0 + nrm(ks[13], (DEPTH, D_MODEL), 0.02),
        "g_ff": 1.0 + nrm(ks[14], (DEPTH, D_MODEL), 0.02),
        "g_ple": 1.0 + nrm(ks[15], (DEPTH, D_MODEL), 0.02),
        "g_final": 1.0 + nrm(ks[16], (D_MODEL,), 0.02),
    }


def reference(x, p, w_in, conv_w, hg_lb_logits, hg_norm_g, att_rel_bias, w_branch, w_o,
              w_ff1, w_ff2, w_ple_in, w_ple_gate, g_mix, g_ff, g_ple, g_final):
    lb_sm = jax.nn.softmax(hg_lb_logits.astype(jnp.float32), axis=0)
    lb_all = jnp.cumsum(lb_sm, axis=0) - lb_sm[0]
    split_points = [int(v) for v in np.cumsum(
        [CONV_WIDTH] * 3 + [HG_WIDTH] * 4 + [ATT_WIDTH] * 3 + [D_MODEL] * 2)]
    h = x
    for l in range(DEPTH):
        xn = rms_norm(h, g_mix[l])
        proj = xn @ w_in[l]
        (cb, cc, ch, hq, hf, hi, hg, aq, ak, av,
         gate_a, gate_b, gate_c) = jnp.split(proj, split_points, axis=-1)
        y_conv = cb * causal_depthwise_conv(cc * ch, conv_w[l])
        y_hg = hgrn2(hq, hf, hi, hg, lb_all[l], hg_norm_g[l])
        y_att = chunk_band_attention(aq, ak, av, att_rel_bias[l])
        merged = (jax.nn.sigmoid(gate_a) * (y_conv @ w_branch[l, 0])
                  + jax.nn.sigmoid(gate_b) * (y_hg @ w_branch[l, 1])
                  + jax.nn.sigmoid(gate_c) * (y_att @ w_branch[l, 2]))
        h = h + merged @ w_o[l]
        hn = rms_norm(h, g_ff[l])
        h = h + jnp.square(jax.nn.relu(hn @ w_ff1[l])) @ w_ff2[l]
        ple_gate = jax.nn.sigmoid(rms_norm(h, g_ple[l]) @ w_ple_gate[l])
        h = h + ple_gate * (p[l] @ w_ple_in[l])
    return rms_norm(h, g_final)
```

```python
import functools

import numpy as np
import jax
import jax.numpy as jnp
from jax import lax
from jax.experimental import pallas as pl
from jax.experimental.pallas import tpu as pltpu

F32 = jnp.float32
BF16 = jnp.bfloat16

EPS = 1e-6
LB_FLOOR = 1e-30
MASK_VALUE = -1e30

MIX_WIDTH = 1024
CONV_K = 3
HEADS = 8
HEAD_DIM = 128
CHUNK = 64
LEFT_CHUNKS = 8
MAX_REL_DIST = 256

HG_BLOCK = 128
HG_LEVELS = 7
ATT_GROUP_CHUNKS = 4
ATT_GQ = ATT_GROUP_CHUNKS * CHUNK
ATT_GK = (ATT_GROUP_CHUNKS + LEFT_CHUNKS) * CHUNK

VMEM_LIMIT_BYTES = 58 * 1024 * 1024

COL_CB, COL_CC, COL_CH, COL_HQ, COL_HI, COL_HG, COL_AQ, COL_AK, COL_AV = range(9)
COL_GATE = 9


def _params(*sem):
    return pltpu.CompilerParams(dimension_semantics=sem, vmem_limit_bytes=VMEM_LIMIT_BYTES)


def _rms(x, g):
    ms = jnp.mean(x * x, axis=-1, keepdims=True)
    return x * lax.rsqrt(ms + EPS) * g


def _dot(a, b):
    return jnp.dot(a, b, preferred_element_type=F32)


def _dot_nt(a, b):
    return lax.dot_general(a, b, (((1,), (1,)), ((), ())), preferred_element_type=F32)


def _dot_tn(a, b):
    return lax.dot_general(a, b, (((0,), (0,)), ((), ())), preferred_element_type=F32)


def _sigmoid(x):
    return 1.0 / (1.0 + jnp.exp(-x))


def _in_proj_kernel(h_ref, g_ref, w_ref, cs_ref, ob_ref, of_ref, xn_ref, *, n_bf16_tiles):
    j = pl.program_id(1)

    @pl.when(j == 0)
    def _():
        xn_ref[...] = _rms(h_ref[...], g_ref[...]).astype(BF16)

    acc = _dot(xn_ref[...], w_ref[...]) * cs_ref[...]

    @pl.when(j < n_bf16_tiles)
    def _():
        ob_ref[...] = acc.astype(BF16)

    @pl.when(j >= n_bf16_tiles)
    def _():
        of_ref[...] = acc


def in_proj(h, g, w, col_scale, n_f32_cols, tm, tn):
    m, d = h.shape
    n = w.shape[1]
    nb = (n - n_f32_cols) // tn
    nf = n_f32_cols // tn
    return pl.pallas_call(
        functools.partial(_in_proj_kernel, n_bf16_tiles=nb),
        grid=(m // tm, nb + nf),
        in_specs=[
            pl.BlockSpec((tm, d), lambda i, j: (i, 0)),
            pl.BlockSpec((1, d), lambda i, j: (0, 0)),
            pl.BlockSpec((d, tn), lambda i, j: (0, j)),
            pl.BlockSpec((1, tn), lambda i, j: (0, j)),
        ],
        out_specs=[
            pl.BlockSpec((tm, tn), lambda i, j: (i, jnp.minimum(j, nb - 1))),
            pl.BlockSpec((tm, tn), lambda i, j: (i, jnp.maximum(j - nb, 0))),
        ],
        out_shape=[
            jax.ShapeDtypeStruct((m, n - n_f32_cols), BF16),
            jax.ShapeDtypeStruct((m, n_f32_cols), F32),
        ],
        scratch_shapes=[pltpu.VMEM((tm, d), BF16)],
        compiler_params=_params("parallel", "arbitrary"),
        name="in_proj",
    )(h, g, w, col_scale)


def _conv_kernel(cb_ref, cc_ref, ch_ref, w_ref, o_ref):
    u = cc_ref[0].astype(F32) * ch_ref[0].astype(F32)
    row = lax.broadcasted_iota(jnp.int32, u.shape, 0)
    u1 = jnp.where(row >= 1, pltpu.roll(u, 1, 0), 0.0)
    u2 = jnp.where(row >= 2, pltpu.roll(u, 2, 0), 0.0)
    w = w_ref[...]
    y = u2 * w[0:1, :] + u1 * w[1:2, :] + u * w[2:3, :]
    o_ref[0] = (cb_ref[0].astype(F32) * y).astype(BF16)


def conv_branch(proj, conv_w, tc):
    b, s, _ = proj.shape
    per = MIX_WIDTH // tc

    def col(block):
        return pl.BlockSpec((1, s, tc), lambda bi, c: (bi, 0, block * per + c))

    return pl.pallas_call(
        _conv_kernel,
        grid=(b, per),
        in_specs=[col(COL_CB), col(COL_CC), col(COL_CH),
                  pl.BlockSpec((CONV_K, tc), lambda bi, c: (0, c))],
        out_specs=pl.BlockSpec((1, s, tc), lambda bi, c: (bi, 0, c)),
        out_shape=jax.ShapeDtypeStruct((b, s, MIX_WIDTH), BF16),
        compiler_params=_params("parallel", "parallel"),
        name="conv_branch",
    )(proj, proj, proj, conv_w)


def _hgrn_level_tables():
    t = np.arange(HG_BLOCK)
    x = t[:, None] ^ t[None, :]
    lvl = np.where(t[:, None] > t[None, :], np.floor(np.log2(np.maximum(x, 1))).astype(np.int64) + 1, -1)
    lvl = np.where(t[:, None] == t[None, :], 0, lvl).astype(np.int32)
    tri = (t[:, None] >= t[None, :]).astype(np.float32)
    return lvl, tri


def _split_point_rows(g, level):
    size, half = 1 << level, 1 << (level - 1)
    n = g.shape[0]
    if size >= 8:
        parts = [jnp.broadcast_to(g[b * size + half - 1:b * size + half, :], (size, g.shape[1]))
                 for b in range(n // size)]
        return jnp.concatenate(parts, axis=0)
    sub = lax.broadcasted_iota(jnp.int32, (8, g.shape[1]), 0)
    parts = []
    for v in range(n // 8):
        acc = None
        for b in range(8 // size):
            r = 8 * v + b * size + half - 1
            cand = jnp.broadcast_to(g[r:r + 1, :], (8, g.shape[1]))
            acc = cand if acc is None else jnp.where(sub >= b * size, cand, acc)
        parts.append(acc)
    return jnp.concatenate(parts, axis=0)


def _hgrn_kernel(q_ref, v_ref, gate_ref, z_ref, lb_ref, ng_ref, tri_ref, lvl_ref, o_ref):
    L = HG_BLOCK
    n_blocks = q_ref.shape[1] // L
    lb = lb_ref[0]
    lb_floor = jnp.maximum(lb, LB_FLOOR)
    one_m_lb = 1.0 - lb
    ng = ng_ref[...]
    tri = tri_ref[...]
    lvl = lvl_ref[...]
    odd = (lax.broadcasted_iota(jnp.int32, (L, HEAD_DIM), 0) & 1) == 1

    def body(n, st):
        r0 = pl.multiple_of(n * L, L)
        rows = pl.ds(r0, L)
        z = z_ref[0, rows, :]
        q = q_ref[0, rows, :].astype(F32)
        v = v_ref[0, rows, :]
        e = jnp.exp(-jnp.abs(z))
        inv = 1.0 / (1.0 + e)
        einv = e * inv
        pos = z >= 0
        f = lb_floor + one_m_lb * jnp.where(pos, inv, einv)
        key = one_m_lb * jnp.where(pos, einv, inv)
        logf = jnp.log(f)
        hi = logf.astype(BF16)
        r1 = logf - hi.astype(F32)
        mid = r1.astype(BF16)
        lo = (r1 - mid.astype(F32)).astype(BF16)
        c3 = _dot(tri, jnp.concatenate([hi, mid, lo], axis=1))
        G = c3[:, :HEAD_DIM] + c3[:, HEAD_DIM:2 * HEAD_DIM] + c3[:, 2 * HEAD_DIM:]

        a = jnp.where(lvl == 0, _dot_nt(q.astype(BF16), key.astype(BF16)), 0.0)
        d1 = jnp.where(odd, f, 1.0)
        a = jnp.where(lvl == 1, _dot_nt((q * d1).astype(BF16), (key * d1).astype(BF16)), a)
        for level in range(2, HG_LEVELS + 1):
            d = jnp.exp(-jnp.abs(G - _split_point_rows(G, level)))
            a = jnp.where(lvl == level, _dot_nt((q * d).astype(BF16), (key * d).astype(BF16)), a)

        g_tot = G[L - 1:L, :]
        o = _dot(a.astype(BF16), v) + _dot_nt((q * jnp.exp(G)).astype(BF16), st.astype(BF16))
        kd = (key * jnp.exp(g_tot - G)).astype(BF16)
        st_new = st * jnp.exp(g_tot) + _dot_tn(v, kd)

        gg = gate_ref[0, rows, :].astype(F32)
        y = _rms(o, ng) * (gg * _sigmoid(gg))
        o_ref[0, rows, :] = y.astype(BF16)
        return st_new

    lax.fori_loop(0, n_blocks, body, jnp.zeros((HEAD_DIM, HEAD_DIM), F32))


def hgrn_branch(proj, z, lb, norm_g):
    b, s, _ = proj.shape
    lvl, tri = _hgrn_level_tables()

    def col(block):
        return pl.BlockSpec((1, s, HEAD_DIM), lambda bi, h: (bi, 0, block * HEADS + h))

    const = lambda bi, h: (0, 0)
    return pl.pallas_call(
        _hgrn_kernel,
        grid=(b, HEADS),
        in_specs=[col(COL_HQ), col(COL_HI), col(COL_HG),
                  pl.BlockSpec((1, s, HEAD_DIM), lambda bi, h: (bi, 0, h)),
                  pl.BlockSpec((1, 1, HEAD_DIM), lambda bi, h: (h, 0, 0)),
                  pl.BlockSpec((1, HEAD_DIM), const),
                  pl.BlockSpec((HG_BLOCK, HG_BLOCK), const),
                  pl.BlockSpec((HG_BLOCK, HG_BLOCK), const)],
        out_specs=pl.BlockSpec((1, s, HEAD_DIM), lambda bi, h: (bi, 0, h)),
        out_shape=jax.ShapeDtypeStruct((b, s, MIX_WIDTH), BF16),
        compiler_params=_params("parallel", "parallel"),
        name="hgrn_branch",
    )(proj, proj, proj, z, lb.reshape(HEADS, 1, HEAD_DIM), norm_g.reshape(1, HEAD_DIM),
      jnp.asarray(tri, BF16), jnp.asarray(lvl))


def _att_group_bias(rel_table):
    qi = np.arange(ATT_GQ) // CHUNK
    a = np.arange(ATT_GQ) % CHUNK
    kj = np.arange(ATT_GK) // CHUNK
    bb = np.arange(ATT_GK) % CHUNK
    band = kj[None, :] - qi[:, None]
    valid = (band >= 0) & (band <= LEFT_CHUNKS)
    rel = (LEFT_CHUNKS - band) * CHUNK + a[:, None] - bb[None, :]
    idx = np.clip(rel, -MAX_REL_DIST, MAX_REL_DIST) + MAX_REL_DIST
    bias = rel_table.astype(F32)[:, idx]
    return jnp.where(jnp.asarray(valid)[None], bias, MASK_VALUE)


def _att_kernel(q_ref, k_ref, v_ref, bias_ref, o_ref):
    n_groups = q_ref.shape[1] // ATT_GQ

    def group(q0, k0, nk):
        q = q_ref[0, pl.ds(q0, ATT_GQ), :]
        k = k_ref[0, pl.ds(k0, nk), :]
        v = v_ref[0, pl.ds(k0, nk), :]
        s = _dot_nt(q, k) + bias_ref[0, :, ATT_GK - nk:]
        m = jnp.max(s, axis=-1, keepdims=True)
        p = jnp.exp(s - m)
        l = jnp.sum(p, axis=-1, keepdims=True)
        o = _dot(p.astype(BF16), v) / l
        o_ref[0, pl.ds(q0, ATT_GQ), :] = o.astype(BF16)

    full_from = LEFT_CHUNKS // ATT_GROUP_CHUNKS
    for g in range(min(full_from, n_groups)):
        group(g * ATT_GQ, 0, (g + 1) * ATT_GQ)

    def body(g, carry):
        q0 = pl.multiple_of(g * ATT_GQ, ATT_GQ)
        k0 = pl.multiple_of((g - full_from) * ATT_GQ, ATT_GQ)
        group(q0, k0, ATT_GK)
        return carry

    lax.fori_loop(full_from, n_groups, body, 0)


def attention_branch(proj, rel_table):
    b, s, _ = proj.shape
    bias = _att_group_bias(rel_table)

    def col(block):
        return pl.BlockSpec((1, s, HEAD_DIM), lambda h, bi: (bi, 0, block * HEADS + h))

    return pl.pallas_call(
        _att_kernel,
        grid=(HEADS, b),
        in_specs=[col(COL_AQ), col(COL_AK), col(COL_AV),
                  pl.BlockSpec((1, ATT_GQ, ATT_GK), lambda h, bi: (h, 0, 0))],
        out_specs=pl.BlockSpec((1, s, HEAD_DIM), lambda h, bi: (bi, 0, h)),
        out_shape=jax.ShapeDtypeStruct((b, s, MIX_WIDTH), BF16),
        compiler_params=_params("parallel", "parallel"),
        name="attention_branch",
    )(proj, proj, proj, bias)


def _merge_kernel(yc_ref, yh_ref, ya_ref, wc_ref, wh_ref, wa_ref, gc_ref, gh_ref, ga_ref, o_ref):
    acc = _sigmoid(gc_ref[...].astype(F32)) * _dot(yc_ref[...], wc_ref[0])
    acc += _sigmoid(gh_ref[...].astype(F32)) * _dot(yh_ref[...], wh_ref[0])
    acc += _sigmoid(ga_ref[...].astype(F32)) * _dot(ya_ref[...], wa_ref[0])
    o_ref[...] = acc.astype(BF16)


def merge_branches(y_conv, y_hg, y_att, w_branch, proj, tm, tn):
    m, kdim = y_conv.shape
    d = w_branch.shape[2]
    per = d // tn

    def y_spec():
        return pl.BlockSpec((tm, kdim), lambda j, i: (i, 0))

    def w_spec(branch):
        return pl.BlockSpec((1, kdim, tn), lambda j, i: (branch, 0, j))

    def gate_spec(branch):
        first = COL_GATE * MIX_WIDTH // tn + branch * per
        return pl.BlockSpec((tm, tn), lambda j, i: (i, first + j))

    return pl.pallas_call(
        _merge_kernel,
        grid=(per, m // tm),
        in_specs=[y_spec(), y_spec(), y_spec(), w_spec(0), w_spec(1), w_spec(2),
                  gate_spec(0), gate_spec(1), gate_spec(2)],
        out_specs=pl.BlockSpec((tm, tn), lambda j, i: (i, j)),
        out_shape=jax.ShapeDtypeStruct((m, d), BF16),
        compiler_params=_params("parallel", "parallel"),
        name="merge_branches",
    )(y_conv, y_hg, y_att, w_branch, w_branch, w_branch, proj, proj, proj)


def _out_proj_kernel(a_ref, w_ref, h_ref, o_ref):
    o_ref[...] = h_ref[...] + _dot(a_ref[...], w_ref[...])


def out_proj(a, w, h, tm):
    m, d = h.shape
    return pl.pallas_call(
        _out_proj_kernel,
        grid=(m // tm,),
        in_specs=[pl.BlockSpec((tm, a.shape[1]), lambda i: (i, 0)),
                  pl.BlockSpec(w.shape, lambda i: (0, 0)),
                  pl.BlockSpec((tm, d), lambda i: (i, 0))],
        out_specs=pl.BlockSpec((tm, d), lambda i: (i, 0)),
        out_shape=jax.ShapeDtypeStruct((m, d), F32),
        compiler_params=_params("parallel"),
        name="out_proj",
    )(a, w, h)


def _ffn_kernel(h_ref, g_ref, w1_ref, w2_ref, o_ref, hn_ref):
    @pl.when(pl.program_id(1) == 0)
    def _():
        x = h_ref[...]
        hn_ref[...] = _rms(x, g_ref[...]).astype(BF16)
        o_ref[...] = x

    hid = jnp.square(jnp.maximum(_dot(hn_ref[...], w1_ref[...]), 0.0))
    o_ref[...] += _dot(hid.astype(BF16), w2_ref[...])


def ffn(h, g, w1, w2, tm, tf):
    m, d = h.shape
    dff = w1.shape[1]
    return pl.pallas_call(
        _ffn_kernel,
        grid=(m // tm, dff // tf),
        in_specs=[pl.BlockSpec((tm, d), lambda i, f: (i, 0)),
                  pl.BlockSpec((1, d), lambda i, f: (0, 0)),
                  pl.BlockSpec((d, tf), lambda i, f: (0, f)),
                  pl.BlockSpec((tf, d), lambda i, f: (f, 0))],
        out_specs=pl.BlockSpec((tm, d), lambda i, f: (i, 0)),
        out_shape=jax.ShapeDtypeStruct((m, d), F32),
        scratch_shapes=[pltpu.VMEM((tm, d), BF16)],
        compiler_params=_params("parallel", "arbitrary"),
        name="ffn",
    )(h, g, w1, w2)


def _ple_kernel(h_ref, g_ref, wg_ref, p_ref, wp_ref, gf_ref, o_ref, *, final_norm):
    x = h_ref[...]
    gate = _sigmoid(_dot(_rms(x, g_ref[...]).astype(BF16), wg_ref[...]))
    out = x + gate * _dot(p_ref[...].astype(BF16), wp_ref[...])
    if final_norm:
        out = _rms(out, gf_ref[...])
    o_ref[...] = out


def ple_gate(h, g, w_gate, p, w_p, g_final, final_norm, tm):
    m, d = h.shape
    const = lambda i: (0, 0)
    return pl.pallas_call(
        functools.partial(_ple_kernel, final_norm=final_norm),
        grid=(m // tm,),
        in_specs=[pl.BlockSpec((tm, d), lambda i: (i, 0)),
                  pl.BlockSpec((1, d), const),
                  pl.BlockSpec(w_gate.shape, const),
                  pl.BlockSpec((tm, p.shape[1]), lambda i: (i, 0)),
                  pl.BlockSpec(w_p.shape, const),
                  pl.BlockSpec((1, d), const)],
        out_specs=pl.BlockSpec((tm, d), lambda i: (i, 0)),
        out_shape=jax.ShapeDtypeStruct((m, d), F32),
        compiler_params=_params("parallel"),
        name="ple_gate",
    )(h, g, w_gate, p, w_p, g_final)


def _tile(n, target):
    t = min(n, target)
    assert n % t == 0, (n, t)
    return t


def kernel(x, p, w_in, conv_w, hg_lb_logits, hg_norm_g, att_rel_bias, w_branch, w_o,
           w_ff1, w_ff2, w_ple_in, w_ple_gate, g_mix, g_ff, g_ple, g_final):
    depth = w_in.shape[0]
    b, s, d = x.shape
    m = b * s
    assert s % ATT_GQ == 0 and s % HG_BLOCK == 0 and d % MIX_WIDTH == 0

    lb_sm = jax.nn.softmax(hg_lb_logits.astype(F32), axis=0)
    lb_all = jnp.cumsum(lb_sm, axis=0) - lb_sm[0]

    w = MIX_WIDTH
    n_in = w_in.shape[2]
    col_scale = jnp.ones((1, n_in), F32).at[:, COL_AQ * w:(COL_AQ + 1) * w].set(HEAD_DIM ** -0.5)

    h = x.reshape(m, d)
    for l in range(depth):
        w_l = w_in[l]
        w_perm = jnp.concatenate([w_l[:, :4 * w], w_l[:, 5 * w:], w_l[:, 4 * w:5 * w]], axis=1).astype(BF16)
        proj, z = in_proj(h, g_mix[l].reshape(1, d), w_perm, col_scale, w,
                          tm=_tile(m, 1024), tn=_tile(w, 1024))
        proj3 = proj.reshape(b, s, proj.shape[1])
        y_conv = conv_branch(proj3, conv_w[l], tc=256)
        y_hg = hgrn_branch(proj3, z.reshape(b, s, w), lb_all[l], hg_norm_g[l])
        y_att = attention_branch(proj3, att_rel_bias[l])
        merged = merge_branches(y_conv.reshape(m, w), y_hg.reshape(m, w), y_att.reshape(m, w),
                                w_branch[l].astype(BF16), proj, tm=_tile(m, 1024), tn=_tile(d, 1024))
        h = out_proj(merged, w_o[l].astype(BF16), h, tm=_tile(m, 512))
        h = ffn(h, g_ff[l].reshape(1, d), w_ff1[l].astype(BF16), w_ff2[l].astype(BF16),
                tm=_tile(m, 1024), tf=_tile(w_ff1.shape[2], 512))
        h = ple_gate(h, g_ple[l].reshape(1, d), w_ple_gate[l].astype(BF16),
                     p[l].reshape(m, p.shape[3]), w_ple_in[l].astype(BF16), g_final.reshape(1, d),
                     final_norm=(l == depth - 1), tm=_tile(m, 512))
    return h.reshape(b, s, d)
```

```python
import functools

import numpy as np
import jax
import jax.numpy as jnp
from jax import lax
from jax.experimental import pallas as pl
from jax.experimental.pallas import tpu as pltpu

F32 = jnp.float32
BF16 = jnp.bfloat16

EPS = 1e-6
LB_FLOOR = 1e-30
MASK_VALUE = -1e30

MIX_WIDTH = 1024
CONV_K = 3
HEADS = 8
HEAD_DIM = 128
CHUNK = 64
LEFT_CHUNKS = 8
MAX_REL_DIST = 256

HG_BLOCK = 128
HG_LEVELS = 7
ATT_GROUP_CHUNKS = 4
ATT_GQ = ATT_GROUP_CHUNKS * CHUNK
ATT_GK = (ATT_GROUP_CHUNKS + LEFT_CHUNKS) * CHUNK
HG_HEADS_PER_STEP = 8

VMEM_LIMIT_BYTES = 58 * 1024 * 1024

COL_CB, COL_CC, COL_CH, COL_HQ, COL_HI, COL_HG, COL_AQ, COL_AK, COL_AV = range(9)
COL_GATE = 9


def _params(*sem):
    return pltpu.CompilerParams(dimension_semantics=sem, vmem_limit_bytes=VMEM_LIMIT_BYTES)


def _rms(x, g):
    ms = jnp.mean(x * x, axis=-1, keepdims=True)
    return x * lax.rsqrt(ms + EPS) * g


def _dot(a, b):
    return jnp.dot(a, b, preferred_element_type=F32)


def _dot_nt(a, b):
    return lax.dot_general(a, b, (((1,), (1,)), ((), ())), preferred_element_type=F32)


def _dot_tn(a, b):
    return lax.dot_general(a, b, (((0,), (0,)), ((), ())), preferred_element_type=F32)


def _sigmoid(x):
    return 1.0 / (1.0 + jnp.exp(-x))


def _in_proj_kernel(h_ref, g_ref, w_ref, cs_ref, ob_ref, of_ref, xn_ref, *, n_bf16_tiles):
    j = pl.program_id(1)

    @pl.when(j == 0)
    def _():
        xn_ref[...] = _rms(h_ref[...], g_ref[...]).astype(BF16)

    acc = _dot(xn_ref[...], w_ref[...]) * cs_ref[...]

    @pl.when(j < n_bf16_tiles)
    def _():
        ob_ref[...] = acc.astype(BF16)

    @pl.when(j >= n_bf16_tiles)
    def _():
        of_ref[...] = acc


def in_proj(h, g, w, col_scale, n_f32_cols, tm, tn):
    m, d = h.shape
    n = w.shape[1]
    nb = (n - n_f32_cols) // tn
    nf = n_f32_cols // tn
    return pl.pallas_call(
        functools.partial(_in_proj_kernel, n_bf16_tiles=nb),
        grid=(m // tm, nb + nf),
        in_specs=[
            pl.BlockSpec((tm, d), lambda i, j: (i, 0)),
            pl.BlockSpec((1, d), lambda i, j: (0, 0)),
            pl.BlockSpec((d, tn), lambda i, j: (0, j)),
            pl.BlockSpec((1, tn), lambda i, j: (0, j)),
        ],
        out_specs=[
            pl.BlockSpec((tm, tn), lambda i, j: (i, jnp.minimum(j, nb - 1))),
            pl.BlockSpec((tm, tn), lambda i, j: (i, jnp.maximum(j - nb, 0))),
        ],
        out_shape=[
            jax.ShapeDtypeStruct((m, n - n_f32_cols), BF16),
            jax.ShapeDtypeStruct((m, n_f32_cols), F32),
        ],
        scratch_shapes=[pltpu.VMEM((tm, d), BF16)],
        compiler_params=_params("parallel", "arbitrary"),
        name="in_proj",
    )(h, g, w, col_scale)


def _conv_kernel(cb_ref, cc_ref, ch_ref, w_ref, o_ref):
    u = cc_ref[0].astype(F32) * ch_ref[0].astype(F32)
    row = lax.broadcasted_iota(jnp.int32, u.shape, 0)
    u1 = jnp.where(row >= 1, pltpu.roll(u, 1, 0), 0.0)
    u2 = jnp.where(row >= 2, pltpu.roll(u, 2, 0), 0.0)
    w = w_ref[...]
    y = u2 * w[0:1, :] + u1 * w[1:2, :] + u * w[2:3, :]
    o_ref[0] = (cb_ref[0].astype(F32) * y).astype(BF16)


def conv_branch(proj, conv_w, tc):
    b, s, _ = proj.shape
    per = MIX_WIDTH // tc

    def col(block):
        return pl.BlockSpec((1, s, tc), lambda bi, c: (bi, 0, block * per + c))

    return pl.pallas_call(
        _conv_kernel,
        grid=(b, per),
        in_specs=[col(COL_CB), col(COL_CC), col(COL_CH),
                  pl.BlockSpec((CONV_K, tc), lambda bi, c: (0, c))],
        out_specs=pl.BlockSpec((1, s, tc), lambda bi, c: (bi, 0, c)),
        out_shape=jax.ShapeDtypeStruct((b, s, MIX_WIDTH), BF16),
        compiler_params=_params("parallel", "parallel"),
        name="conv_branch",
    )(proj, proj, proj, conv_w)


def _hgrn_level_tables():
    t = np.arange(HG_BLOCK)
    x = t[:, None] ^ t[None, :]
    lvl = np.where(t[:, None] > t[None, :], np.floor(np.log2(np.maximum(x, 1))).astype(np.int64) + 1, -1)
    lvl = np.where(t[:, None] == t[None, :], 0, lvl).astype(np.int32)
    tri = (t[:, None] >= t[None, :]).astype(np.float32)
    return lvl, tri


def _split_point_rows(g, level):
    size, half = 1 << level, 1 << (level - 1)
    n = g.shape[0]
    if size >= 8:
        parts = [jnp.broadcast_to(g[b * size + half - 1:b * size + half, :], (size, g.shape[1]))
                 for b in range(n // size)]
        return jnp.concatenate(parts, axis=0)
    sub = lax.broadcasted_iota(jnp.int32, (8, g.shape[1]), 0)
    parts = []
    for v in range(n // 8):
        acc = None
        for b in range(8 // size):
            r = 8 * v + b * size + half - 1
            cand = jnp.broadcast_to(g[r:r + 1, :], (8, g.shape[1]))
            acc = cand if acc is None else jnp.where(sub >= b * size, cand, acc)
        parts.append(acc)
    return jnp.concatenate(parts, axis=0)


def _neg_abs(x):
    bits = lax.bitcast_convert_type(x, jnp.uint32) | jnp.uint32(0x80000000)
    return lax.bitcast_convert_type(bits, F32)


def _hgrn_head_block(q, v, gate, z, lb_floor, one_m_lb, ng, tri, lvl, odd, st):
    L = HG_BLOCK
    e = jnp.exp(_neg_abs(z))
    inv = 1.0 / (1.0 + e)
    einv = e * inv
    pos = z >= 0
    f = lb_floor + one_m_lb * jnp.where(pos, inv, einv)
    key = one_m_lb * jnp.where(pos, einv, inv)
    lf2 = jnp.log2(f)
    hi = lf2.astype(BF16)
    lo = (lf2 - hi.astype(F32)).astype(BF16)
    c2 = _dot(tri, jnp.concatenate([hi, lo], axis=1))
    g2 = c2[:, :HEAD_DIM] + c2[:, HEAD_DIM:]

    a = jnp.where(lvl == 0, _dot_nt(q.astype(BF16), key.astype(BF16)), 0.0)
    d1 = jnp.where(odd, f, 1.0)
    a = jnp.where(lvl == 1, _dot_nt((q * d1).astype(BF16), (key * d1).astype(BF16)), a)
    for level in range(2, HG_LEVELS + 1):
        d = jnp.exp2(_neg_abs(g2 - _split_point_rows(g2, level)))
        a = jnp.where(lvl == level, _dot_nt((q * d).astype(BF16), (key * d).astype(BF16)), a)

    g_tot = g2[L - 1:L, :]
    o = _dot(a.astype(BF16), v) + _dot_nt((q * jnp.exp2(g2)).astype(BF16), st.astype(BF16))
    kd = (key * jnp.exp2(g_tot - g2)).astype(BF16)
    st_new = st * jnp.exp2(g_tot) + _dot_tn(v, kd)
    y = _rms(o, ng) * (gate * _sigmoid(gate))
    return y, st_new


def _hgrn_kernel(q_ref, v_ref, gate_ref, z_ref, lb_ref, ng_ref, tri_ref, lvl_ref, o_ref, st_ref):
    L = HG_BLOCK
    n_blocks = q_ref.shape[1] // L
    n_heads = q_ref.shape[2] // HEAD_DIM
    lb = lb_ref[0]
    lb_floor = jnp.maximum(lb, LB_FLOOR)
    one_m_lb = 1.0 - lb
    ng = ng_ref[...]
    tri = tri_ref[...]
    lvl = lvl_ref[...]
    odd = (lax.broadcasted_iota(jnp.int32, (L, HEAD_DIM), 0) & 1) == 1
    st_ref[...] = jnp.zeros(st_ref.shape, F32)

    def body(n, carry):
        rows = pl.ds(pl.multiple_of(n * L, L), L)
        for hh in range(n_heads):
            cols = slice(hh * HEAD_DIM, (hh + 1) * HEAD_DIM)
            y, st_new = _hgrn_head_block(
                q_ref[0, rows, cols].astype(F32), v_ref[0, rows, cols], gate_ref[0, rows, cols].astype(F32),
                z_ref[0, rows, cols], lb_floor[:, cols], one_m_lb[:, cols], ng, tri, lvl, odd, st_ref[hh])
            st_ref[hh] = st_new
            o_ref[0, rows, cols] = y.astype(BF16)
        return carry

    lax.fori_loop(0, n_blocks, body, 0)


def hgrn_branch(proj, z, lb, norm_g, heads_per_step):
    b, s, _ = proj.shape
    lvl, tri = _hgrn_level_tables()
    width = heads_per_step * HEAD_DIM
    per = MIX_WIDTH // width

    def col(block):
        return pl.BlockSpec((1, s, width), lambda bi, h: (bi, 0, block * per + h))

    const = lambda bi, h: (0, 0)
    return pl.pallas_call(
        _hgrn_kernel,
        grid=(b, per),
        in_specs=[col(COL_HQ), col(COL_HI), col(COL_HG),
                  pl.BlockSpec((1, s, width), lambda bi, h: (bi, 0, h)),
                  pl.BlockSpec((1, 1, width), lambda bi, h: (h, 0, 0)),
                  pl.BlockSpec((1, HEAD_DIM), const),
                  pl.BlockSpec((HG_BLOCK, HG_BLOCK), const),
                  pl.BlockSpec((HG_BLOCK, HG_BLOCK), const)],
        out_specs=pl.BlockSpec((1, s, width), lambda bi, h: (bi, 0, h)),
        out_shape=jax.ShapeDtypeStruct((b, s, MIX_WIDTH), BF16),
        scratch_shapes=[pltpu.VMEM((heads_per_step, HEAD_DIM, HEAD_DIM), F32)],
        compiler_params=_params("parallel", "parallel"),
        name="hgrn_branch",
    )(proj, proj, proj, z, lb.reshape(per, 1, width), norm_g.reshape(1, HEAD_DIM),
      jnp.asarray(tri, BF16), jnp.asarray(lvl))


def _att_group_bias(rel_table):
    heads = rel_table.shape[0]
    table = rel_table.astype(F32)
    band = np.arange(ATT_GK)[None, :] // CHUNK - np.arange(ATT_GQ)[:, None] // CHUNK
    valid = (band >= 0) & (band <= LEFT_CHUNKS)
    lo, hi = LEFT_CHUNKS * CHUNK - (ATT_GK - 1), LEFT_CHUNKS * CHUNK + ATT_GQ - 1
    assert -MAX_REL_DIST <= lo and hi >= MAX_REL_DIST
    n_rel = hi - lo + 1
    w = jnp.concatenate([table[:, lo + MAX_REL_DIST:],
                         jnp.broadcast_to(table[:, 2 * MAX_REL_DIST:], (heads, hi - MAX_REL_DIST))], axis=1)
    m = jnp.tile(w, (1, ATT_GQ + 1))[:, :ATT_GQ * (n_rel + 1)].reshape(heads, ATT_GQ, n_rel + 1)
    bias = m[:, :, :ATT_GK][:, :, ::-1]
    return jnp.where(jnp.asarray(valid)[None], bias, MASK_VALUE)


def _att_kernel(q_ref, k_ref, v_ref, bias_ref, o_ref):
    n_groups = q_ref.shape[1] // ATT_GQ

    for g in range(n_groups):
        q0 = g * ATT_GQ
        k0 = max(0, q0 + ATT_GQ - ATT_GK)
        nk = q0 + ATT_GQ - k0
        q = q_ref[0, q0:q0 + ATT_GQ, :]
        k = k_ref[0, k0:k0 + nk, :]
        v = v_ref[0, k0:k0 + nk, :]
        s = _dot_nt(q, k) + bias_ref[0, :, ATT_GK - nk:]
        m = jnp.max(s, axis=-1, keepdims=True)
        p = jnp.exp(s - m)
        l = jnp.sum(p, axis=-1, keepdims=True)
        o = _dot(p.astype(BF16), v) / l
        o_ref[0, q0:q0 + ATT_GQ, :] = o.astype(BF16)


def attention_branch(proj, rel_table):
    b, s, _ = proj.shape
    bias = _att_group_bias(rel_table)

    def col(block):
        return pl.BlockSpec((1, s, HEAD_DIM), lambda h, bi: (bi, 0, block * HEADS + h))

    return pl.pallas_call(
        _att_kernel,
        grid=(HEADS, b),
        in_specs=[col(COL_AQ), col(COL_AK), col(COL_AV),
                  pl.BlockSpec((1, ATT_GQ, ATT_GK), lambda h, bi: (h, 0, 0))],
        out_specs=pl.BlockSpec((1, s, HEAD_DIM), lambda h, bi: (bi, 0, h)),
        out_shape=jax.ShapeDtypeStruct((b, s, MIX_WIDTH), BF16),
        compiler_params=_params("parallel", "parallel"),
        name="attention_branch",
    )(proj, proj, proj, bias)


def _merge_kernel(yc_ref, yh_ref, ya_ref, wc_ref, wh_ref, wa_ref, gc_ref, gh_ref, ga_ref, o_ref):
    acc = _sigmoid(gc_ref[...].astype(F32)) * _dot(yc_ref[...], wc_ref[0])
    acc += _sigmoid(gh_ref[...].astype(F32)) * _dot(yh_ref[...], wh_ref[0])
    acc += _sigmoid(ga_ref[...].astype(F32)) * _dot(ya_ref[...], wa_ref[0])
    o_ref[...] = acc.astype(BF16)


def merge_branches(y_conv, y_hg, y_att, w_branch, proj, tm, tn):
    m, kdim = y_conv.shape
    d = w_branch.shape[2]
    per = d // tn

    def y_spec():
        return pl.BlockSpec((tm, kdim), lambda j, i: (i, 0))

    def w_spec(branch):
        return pl.BlockSpec((1, kdim, tn), lambda j, i: (branch, 0, j))

    def gate_spec(branch):
        first = COL_GATE * MIX_WIDTH // tn + branch * per
        return pl.BlockSpec((tm, tn), lambda j, i: (i, first + j))

    return pl.pallas_call(
        _merge_kernel,
        grid=(per, m // tm),
        in_specs=[y_spec(), y_spec(), y_spec(), w_spec(0), w_spec(1), w_spec(2),
                  gate_spec(0), gate_spec(1), gate_spec(2)],
        out_specs=pl.BlockSpec((tm, tn), lambda j, i: (i, j)),
        out_shape=jax.ShapeDtypeStruct((m, d), BF16),
        compiler_params=_params("parallel", "parallel"),
        name="merge_branches",
    )(y_conv, y_hg, y_att, w_branch, w_branch, w_branch, proj, proj, proj)


def _out_proj_kernel(a_ref, w_ref, h_ref, o_ref):
    o_ref[...] = h_ref[...] + _dot(a_ref[...], w_ref[...])


def out_proj(a, w, h, tm):
    m, d = h.shape
    return pl.pallas_call(
        _out_proj_kernel,
        grid=(m // tm,),
        in_specs=[pl.BlockSpec((tm, a.shape[1]), lambda i: (i, 0)),
                  pl.BlockSpec(w.shape, lambda i: (0, 0)),
                  pl.BlockSpec((tm, d), lambda i: (i, 0))],
        out_specs=pl.BlockSpec((tm, d), lambda i: (i, 0)),
        out_shape=jax.ShapeDtypeStruct((m, d), F32),
        compiler_params=_params("parallel"),
        name="out_proj",
    )(a, w, h)


def _ffn_kernel(h_ref, g_ref, w1_ref, w2_ref, o_ref, hn_ref):
    @pl.when(pl.program_id(1) == 0)
    def _():
        x = h_ref[...]
        hn_ref[...] = _rms(x, g_ref[...]).astype(BF16)
        o_ref[...] = x

    hid = jnp.square(jnp.maximum(_dot(hn_ref[...], w1_ref[...]), 0.0))
    o_ref[...] += _dot(hid.astype(BF16), w2_ref[...])


def ffn(h, g, w1, w2, tm, tf):
    m, d = h.shape
    dff = w1.shape[1]
    return pl.pallas_call(
        _ffn_kernel,
        grid=(m // tm, dff // tf),
        in_specs=[pl.BlockSpec((tm, d), lambda i, f: (i, 0)),
                  pl.BlockSpec((1, d), lambda i, f: (0, 0)),
                  pl.BlockSpec((d, tf), lambda i, f: (0, f)),
                  pl.BlockSpec((tf, d), lambda i, f: (f, 0))],
        out_specs=pl.BlockSpec((tm, d), lambda i, f: (i, 0)),
        out_shape=jax.ShapeDtypeStruct((m, d), F32),
        scratch_shapes=[pltpu.VMEM((tm, d), BF16)],
        compiler_params=_params("parallel", "arbitrary"),
        name="ffn",
    )(h, g, w1, w2)


def _ple_kernel(h_ref, g_ref, wg_ref, p_ref, wp_ref, gf_ref, o_ref, *, final_norm):
    x = h_ref[...]
    gate = _sigmoid(_dot(_rms(x, g_ref[...]).astype(BF16), wg_ref[...]))
    out = x + gate * _dot(p_ref[0].astype(BF16), wp_ref[...])
    if final_norm:
        out = _rms(out, gf_ref[...])
    o_ref[...] = out


def ple_gate(h, g, w_gate, p, layer, w_p, g_final, final_norm, tm):
    m, d = h.shape
    const = lambda i: (0, 0)
    return pl.pallas_call(
        functools.partial(_ple_kernel, final_norm=final_norm),
        grid=(m // tm,),
        in_specs=[pl.BlockSpec((tm, d), lambda i: (i, 0)),
                  pl.BlockSpec((1, d), const),
                  pl.BlockSpec(w_gate.shape, const),
                  pl.BlockSpec((1, tm, p.shape[2]), lambda i: (layer, i, 0)),
                  pl.BlockSpec(w_p.shape, const),
                  pl.BlockSpec((1, d), const)],
        out_specs=pl.BlockSpec((tm, d), lambda i: (i, 0)),
        out_shape=jax.ShapeDtypeStruct((m, d), F32),
        compiler_params=_params("parallel"),
        name="ple_gate",
    )(h, g, w_gate, p, w_p, g_final)


def _tile(n, target):
    t = min(n, target)
    assert n % t == 0, (n, t)
    return t


def kernel(x, p, w_in, conv_w, hg_lb_logits, hg_norm_g, att_rel_bias, w_branch, w_o,
           w_ff1, w_ff2, w_ple_in, w_ple_gate, g_mix, g_ff, g_ple, g_final):
    depth = w_in.shape[0]
    b, s, d = x.shape
    m = b * s
    assert s % ATT_GQ == 0 and s % HG_BLOCK == 0 and d % MIX_WIDTH == 0

    lb_sm = jax.nn.softmax(hg_lb_logits.astype(F32), axis=0)
    lb_all = jnp.cumsum(lb_sm, axis=0) - lb_sm[0]

    w = MIX_WIDTH
    n_in = w_in.shape[2]
    col_scale = jnp.ones((1, n_in), F32).at[:, COL_AQ * w:(COL_AQ + 1) * w].set(HEAD_DIM ** -0.5)

    h = x.reshape(m, d)
    for l in range(depth):
        w_l = w_in[l]
        w_perm = jnp.concatenate([w_l[:, :4 * w], w_l[:, 5 * w:], w_l[:, 4 * w:5 * w]], axis=1).astype(BF16)
        proj, z = in_proj(h, g_mix[l].reshape(1, d), w_perm, col_scale, w,
                          tm=_tile(m, 1024), tn=_tile(w, 1024))
        proj3 = proj.reshape(b, s, proj.shape[1])
        y_conv = conv_branch(proj3, conv_w[l], tc=256)
        y_hg = hgrn_branch(proj3, z.reshape(b, s, w), lb_all[l], hg_norm_g[l], HG_HEADS_PER_STEP)
        y_att = attention_branch(proj3, att_rel_bias[l])
        merged = merge_branches(y_conv.reshape(m, w), y_hg.reshape(m, w), y_att.reshape(m, w),
                                w_branch[l].astype(BF16), proj, tm=_tile(m, 1024), tn=_tile(d, 1024))
        h = out_proj(merged, w_o[l].astype(BF16), h, tm=_tile(m, 512))
        h = ffn(h, g_ff[l].reshape(1, d), w_ff1[l].astype(BF16), w_ff2[l].astype(BF16),
                tm=_tile(m, 1024), tf=_tile(w_ff1.shape[2], 512))
        h = ple_gate(h, g_ple[l].reshape(1, d), w_ple_gate[l].astype(BF16),
                     p.reshape(depth, m, p.shape[3]), l, w_ple_in[l].astype(BF16), g_final.reshape(1, d),
                     final_norm=(l == depth - 1), tm=_tile(m, 512))
    return h.reshape(b, s, d)
```

```python
import functools

import numpy as np
import jax
import jax.numpy as jnp
from jax import lax
from jax.experimental import pallas as pl
from jax.experimental.pallas import tpu as pltpu

F32 = jnp.float32
BF16 = jnp.bfloat16

EPS = 1e-6
LB_FLOOR = 1e-30
MASK_VALUE = -1e30
LOG2_E = 1.4426950408889634

MIX_WIDTH = 1024
CONV_K = 3
HEADS = 8
HEAD_DIM = 128
CHUNK = 64
LEFT_CHUNKS = 8
MAX_REL_DIST = 256

HG_BLOCK = 128
HG_LEVELS = 7
ATT_GROUP_CHUNKS = 4
ATT_GQ = ATT_GROUP_CHUNKS * CHUNK
ATT_GK = (ATT_GROUP_CHUNKS + LEFT_CHUNKS) * CHUNK
HG_HEADS_PER_STEP = 8

VMEM_LIMIT_BYTES = 58 * 1024 * 1024

COL_CB, COL_CC, COL_CH, COL_HQ, COL_HI, COL_HG, COL_AQ, COL_AK, COL_AV = range(9)
COL_GATE = 9


def _params(*sem):
    return pltpu.CompilerParams(dimension_semantics=sem, vmem_limit_bytes=VMEM_LIMIT_BYTES)


def _rms(x, g):
    ms = jnp.mean(x * x, axis=-1, keepdims=True)
    return x * lax.rsqrt(ms + EPS) * g


def _dot(a, b):
    return jnp.dot(a, b, preferred_element_type=F32)


def _dot_nt(a, b):
    return lax.dot_general(a, b, (((1,), (1,)), ((), ())), preferred_element_type=F32)


def _dot_tn(a, b):
    return lax.dot_general(a, b, (((0,), (0,)), ((), ())), preferred_element_type=F32)


def _sigmoid(x):
    return 1.0 / (1.0 + jnp.exp(-x))


def _in_proj_kernel(h_ref, g_ref, w_ref, ob_ref, of_ref, xn_ref, *, f32_tile):
    j = pl.program_id(1)

    @pl.when(j == 0)
    def _():
        xn_ref[...] = _rms(h_ref[...], g_ref[...]).astype(BF16)

    @pl.when(j != f32_tile)
    def _():
        ob_ref[...] = _dot(xn_ref[...], w_ref[0]).astype(BF16)

    @pl.when(j == f32_tile)
    def _():
        of_ref[...] = _dot(xn_ref[...], w_ref[0])


def in_proj(h, g, w, layer, f32_tile, tm, tn):
    m, d = h.shape
    n = w.shape[2]
    bf16_block = lambda j: jnp.where(j < f32_tile, j, jnp.maximum(j - 1, f32_tile - 1))
    return pl.pallas_call(
        functools.partial(_in_proj_kernel, f32_tile=f32_tile),
        grid=(m // tm, n // tn),
        in_specs=[
            pl.BlockSpec((tm, d), lambda i, j: (i, 0)),
            pl.BlockSpec((1, d), lambda i, j: (0, 0)),
            pl.BlockSpec((1, d, tn), lambda i, j: (layer, 0, j)),
        ],
        out_specs=[
            pl.BlockSpec((tm, tn), lambda i, j: (i, bf16_block(j))),
            pl.BlockSpec((tm, tn), lambda i, j: (i, 0)),
        ],
        out_shape=[
            jax.ShapeDtypeStruct((m, n - tn), BF16),
            jax.ShapeDtypeStruct((m, tn), F32),
        ],
        scratch_shapes=[pltpu.VMEM((tm, d), BF16)],
        compiler_params=_params("parallel", "arbitrary"),
        name="in_proj",
    )(h, g, w)


def _conv_kernel(cb_ref, cc_ref, ch_ref, w_ref, o_ref):
    u = cc_ref[0].astype(F32) * ch_ref[0].astype(F32)
    row = lax.broadcasted_iota(jnp.int32, u.shape, 0)
    u1 = jnp.where(row >= 1, pltpu.roll(u, 1, 0), 0.0)
    u2 = jnp.where(row >= 2, pltpu.roll(u, 2, 0), 0.0)
    w = w_ref[...]
    y = u2 * w[0:1, :] + u1 * w[1:2, :] + u * w[2:3, :]
    o_ref[0] = (cb_ref[0].astype(F32) * y).astype(BF16)


def conv_branch(proj, conv_w, tc):
    b, s, _ = proj.shape
    per = MIX_WIDTH // tc

    def col(block):
        return pl.BlockSpec((1, s, tc), lambda bi, c: (bi, 0, block * per + c))

    return pl.pallas_call(
        _conv_kernel,
        grid=(b, per),
        in_specs=[col(COL_CB), col(COL_CC), col(COL_CH),
                  pl.BlockSpec((CONV_K, tc), lambda bi, c: (0, c))],
        out_specs=pl.BlockSpec((1, s, tc), lambda bi, c: (bi, 0, c)),
        out_shape=jax.ShapeDtypeStruct((b, s, MIX_WIDTH), BF16),
        compiler_params=_params("parallel", "parallel"),
        name="conv_branch",
    )(proj, proj, proj, conv_w)


def _hgrn_level_tables():
    t = np.arange(HG_BLOCK)
    x = t[:, None] ^ t[None, :]
    lvl = np.where(t[:, None] > t[None, :], np.floor(np.log2(np.maximum(x, 1))).astype(np.int64) + 1, -1)
    lvl = np.where(t[:, None] == t[None, :], 0, lvl).astype(np.int32)
    tri = (t[:, None] >= t[None, :]).astype(np.float32)
    return lvl, tri


def _split_point_rows(g, level):
    size, half = 1 << level, 1 << (level - 1)
    n = g.shape[0]
    if size >= 8:
        parts = [jnp.broadcast_to(g[b * size + half - 1:b * size + half, :], (size, g.shape[1]))
                 for b in range(n // size)]
        return jnp.concatenate(parts, axis=0)
    sub = lax.broadcasted_iota(jnp.int32, (8, g.shape[1]), 0)
    parts = []
    for v in range(n // 8):
        acc = None
        for b in range(8 // size):
            r = 8 * v + b * size + half - 1
            cand = jnp.broadcast_to(g[r:r + 1, :], (8, g.shape[1]))
            acc = cand if acc is None else jnp.where(sub >= b * size, cand, acc)
        parts.append(acc)
    return jnp.concatenate(parts, axis=0)


def _query_or_key_rows(q, key, level):
    size, half = 1 << level, 1 << (level - 1)
    n = q.shape[0]
    if half >= 8:
        parts = []
        for b in range(n // size):
            parts += [key[b * size:b * size + half], q[b * size + half:(b + 1) * size]]
        return jnp.concatenate(parts, axis=0)
    row = lax.broadcasted_iota(jnp.int32, q.shape, 0)
    return jnp.where((row & half) != 0, q, key)


def _neg_abs(x):
    bits = lax.bitcast_convert_type(x, jnp.uint32) | jnp.uint32(0x80000000)
    return lax.bitcast_convert_type(bits, F32)


def _hgrn_head_block(q, v, gate, z, lb_floor, one_m_lb, ng, tri, lvl, st):
    L = HG_BLOCK
    e = jnp.exp(_neg_abs(z))
    inv = 1.0 / (1.0 + e)
    einv = e * inv
    pos = z >= 0
    f = lb_floor + one_m_lb * jnp.where(pos, inv, einv)
    key = one_m_lb * jnp.where(pos, einv, inv)
    lf2 = jnp.log2(f)
    hi = lf2.astype(BF16)
    lo = (lf2 - hi.astype(F32)).astype(BF16)
    c2 = _dot(tri, jnp.concatenate([hi, lo], axis=1))
    g2 = c2[:, :HEAD_DIM] + c2[:, HEAD_DIM:]

    p01 = _dot_nt(jnp.concatenate([q, q * f], axis=0).astype(BF16), key.astype(BF16))
    a = jnp.where(lvl == 0, p01[:L], 0.0)
    a = jnp.where(lvl == 1, p01[L:], a)
    for level in range(2, HG_LEVELS + 1):
        d = jnp.exp2(_neg_abs(g2 - _split_point_rows(g2, level)))
        x = (_query_or_key_rows(q, key, level) * d).astype(BF16)
        a = jnp.where(lvl == level, _dot_nt(x, x), a)

    g_tot = g2[L - 1:L, :]
    o = _dot(a.astype(BF16), v) + _dot_nt((q * jnp.exp2(g2)).astype(BF16), st.astype(BF16))
    kd = (key * jnp.exp2(g_tot - g2)).astype(BF16)
    st_new = st * jnp.exp2(g_tot) + _dot_tn(v, kd)
    y = _rms(o, ng) * (gate * _sigmoid(gate))
    return y, st_new


def _hgrn_kernel(q_ref, v_ref, gate_ref, z_ref, lb_ref, ng_ref, tri_ref, lvl_ref, o_ref, st_ref):
    L = HG_BLOCK
    n_blocks = q_ref.shape[1] // L
    n_heads = q_ref.shape[2] // HEAD_DIM
    lb = lb_ref[0]
    lb_floor = jnp.maximum(lb, LB_FLOOR)
    one_m_lb = 1.0 - lb
    ng = ng_ref[...]
    tri = tri_ref[...]
    lvl = lvl_ref[...]
    st_ref[...] = jnp.zeros(st_ref.shape, F32)

    def body(n, carry):
        rows = pl.ds(pl.multiple_of(n * L, L), L)
        for hh in range(n_heads):
            cols = slice(hh * HEAD_DIM, (hh + 1) * HEAD_DIM)
            y, st_new = _hgrn_head_block(
                q_ref[0, rows, cols].astype(F32), v_ref[0, rows, cols], gate_ref[0, rows, cols].astype(F32),
                z_ref[0, rows, cols], lb_floor[:, cols], one_m_lb[:, cols], ng, tri, lvl, st_ref[hh])
            st_ref[hh] = st_new
            o_ref[0, rows, cols] = y.astype(BF16)
        return carry

    lax.fori_loop(0, n_blocks, body, 0)


def hgrn_branch(proj, z, lb, norm_g, heads_per_step):
    b, s, _ = proj.shape
    lvl, tri = _hgrn_level_tables()
    width = heads_per_step * HEAD_DIM
    per = MIX_WIDTH // width

    def col(block):
        return pl.BlockSpec((1, s, width), lambda bi, h: (bi, 0, block * per + h))

    const = lambda bi, h: (0, 0)
    return pl.pallas_call(
        _hgrn_kernel,
        grid=(b, per),
        in_specs=[col(COL_HQ), col(COL_HI), col(COL_HG),
                  pl.BlockSpec((1, s, width), lambda bi, h: (bi, 0, h)),
                  pl.BlockSpec((1, 1, width), lambda bi, h: (h, 0, 0)),
                  pl.BlockSpec((1, HEAD_DIM), const),
                  pl.BlockSpec((HG_BLOCK, HG_BLOCK), const),
                  pl.BlockSpec((HG_BLOCK, HG_BLOCK), const)],
        out_specs=pl.BlockSpec((1, s, width), lambda bi, h: (bi, 0, h)),
        out_shape=jax.ShapeDtypeStruct((b, s, MIX_WIDTH), BF16),
        scratch_shapes=[pltpu.VMEM((heads_per_step, HEAD_DIM, HEAD_DIM), F32)],
        compiler_params=_params("parallel", "parallel"),
        name="hgrn_branch",
    )(proj, proj, proj, z, lb.reshape(per, 1, width), norm_g.reshape(1, HEAD_DIM),
      jnp.asarray(tri, BF16), jnp.asarray(lvl))


def _att_group_bias(rel_table):
    heads = rel_table.shape[0]
    table = rel_table.astype(F32) * LOG2_E
    band = np.arange(ATT_GK)[None, :] // CHUNK - np.arange(ATT_GQ)[:, None] // CHUNK
    valid = (band >= 0) & (band <= LEFT_CHUNKS)
    lo, hi = LEFT_CHUNKS * CHUNK - (ATT_GK - 1), LEFT_CHUNKS * CHUNK + ATT_GQ - 1
    assert -MAX_REL_DIST <= lo and hi >= MAX_REL_DIST
    period = hi - lo + 2
    wr = jnp.concatenate([jnp.broadcast_to(table[:, 2 * MAX_REL_DIST:], (heads, hi - MAX_REL_DIST)),
                          jnp.flip(table[:, lo + MAX_REL_DIST:], axis=1),
                          jnp.zeros((heads, 1), F32)], axis=1)
    m = jnp.tile(wr, (1, ATT_GQ))[:, :ATT_GQ * (period - 1)].reshape(heads, ATT_GQ, period - 1)
    bias = m[:, :, ATT_GQ - 1:ATT_GQ - 1 + ATT_GK]
    return jnp.where(jnp.asarray(valid)[None], bias, MASK_VALUE)


def _att_kernel(q_ref, k_ref, v_ref, bias_ref, o_ref):
    n_groups = q_ref.shape[1] // ATT_GQ

    for g in range(n_groups):
        q0 = g * ATT_GQ
        k0 = max(0, q0 + ATT_GQ - ATT_GK)
        nk = q0 + ATT_GQ - k0
        q = q_ref[0, q0:q0 + ATT_GQ, :]
        k = k_ref[0, k0:k0 + nk, :]
        v = v_ref[0, k0:k0 + nk, :]
        s = _dot_nt(q, k) + bias_ref[0, :, ATT_GK - nk:]
        m = jnp.max(s, axis=-1, keepdims=True)
        p = jnp.exp2(s - m)
        l = jnp.sum(p, axis=-1, keepdims=True)
        o = _dot(p.astype(BF16), v) / l
        o_ref[0, q0:q0 + ATT_GQ, :] = o.astype(BF16)


def attention_branch(proj, rel_table):
    b, s, _ = proj.shape
    bias = _att_group_bias(rel_table)

    def col(block):
        return pl.BlockSpec((1, s, HEAD_DIM), lambda h, bi: (bi, 0, block * HEADS + h))

    return pl.pallas_call(
        _att_kernel,
        grid=(HEADS, b),
        in_specs=[col(COL_AQ), col(COL_AK), col(COL_AV),
                  pl.BlockSpec((1, ATT_GQ, ATT_GK), lambda h, bi: (h, 0, 0))],
        out_specs=pl.BlockSpec((1, s, HEAD_DIM), lambda h, bi: (bi, 0, h)),
        out_shape=jax.ShapeDtypeStruct((b, s, MIX_WIDTH), BF16),
        compiler_params=_params("parallel", "parallel"),
        name="attention_branch",
    )(proj, proj, proj, bias)


def _merge_kernel(yc_ref, yh_ref, ya_ref, wc_ref, wh_ref, wa_ref, gc_ref, gh_ref, ga_ref, o_ref):
    acc = _sigmoid(gc_ref[...].astype(F32)) * _dot(yc_ref[...], wc_ref[0, 0])
    acc += _sigmoid(gh_ref[...].astype(F32)) * _dot(yh_ref[...], wh_ref[0, 0])
    acc += _sigmoid(ga_ref[...].astype(F32)) * _dot(ya_ref[...], wa_ref[0, 0])
    o_ref[...] = acc.astype(BF16)


def merge_branches(y_conv, y_hg, y_att, w_branch, layer, proj, tm, tn):
    m, kdim = y_conv.shape
    d = w_branch.shape[3]
    per = d // tn

    def y_spec():
        return pl.BlockSpec((tm, kdim), lambda j, i: (i, 0))

    def w_spec(branch):
        return pl.BlockSpec((1, 1, kdim, tn), lambda j, i: (layer, branch, 0, j))

    def gate_spec(branch):
        first = COL_GATE * MIX_WIDTH // tn + branch * per
        return pl.BlockSpec((tm, tn), lambda j, i: (i, first + j))

    return pl.pallas_call(
        _merge_kernel,
        grid=(per, m // tm),
        in_specs=[y_spec(), y_spec(), y_spec(), w_spec(0), w_spec(1), w_spec(2),
                  gate_spec(0), gate_spec(1), gate_spec(2)],
        out_specs=pl.BlockSpec((tm, tn), lambda j, i: (i, j)),
        out_shape=jax.ShapeDtypeStruct((m, d), BF16),
        compiler_params=_params("parallel", "parallel"),
        name="merge_branches",
    )(y_conv, y_hg, y_att, w_branch, w_branch, w_branch, proj, proj, proj)


def _out_proj_kernel(a_ref, w_ref, h_ref, o_ref):
    o_ref[...] = h_ref[...] + _dot(a_ref[...], w_ref[0])


def out_proj(a, w, layer, h, tm):
    m, d = h.shape
    return pl.pallas_call(
        _out_proj_kernel,
        grid=(m // tm,),
        in_specs=[pl.BlockSpec((tm, a.shape[1]), lambda i: (i, 0)),
                  pl.BlockSpec((1,) + w.shape[1:], lambda i: (layer, 0, 0)),
                  pl.BlockSpec((tm, d), lambda i: (i, 0))],
        out_specs=pl.BlockSpec((tm, d), lambda i: (i, 0)),
        out_shape=jax.ShapeDtypeStruct((m, d), F32),
        compiler_params=_params("parallel"),
        name="out_proj",
    )(a, w, h)


def _ffn_kernel(h_ref, g_ref, w1_ref, w2_ref, o_ref, hn_ref):
    @pl.when(pl.program_id(1) == 0)
    def _():
        x = h_ref[...]
        hn_ref[...] = _rms(x, g_ref[...]).astype(BF16)
        o_ref[...] = x

    hid = jnp.square(jnp.maximum(_dot(hn_ref[...], w1_ref[0]), 0.0))
    o_ref[...] += _dot(hid.astype(BF16), w2_ref[0])


def ffn(h, g, w1, w2, layer, tm, tf):
    m, d = h.shape
    dff = w1.shape[2]
    return pl.pallas_call(
        _ffn_kernel,
        grid=(m // tm, dff // tf),
        in_specs=[pl.BlockSpec((tm, d), lambda i, f: (i, 0)),
                  pl.BlockSpec((1, d), lambda i, f: (0, 0)),
                  pl.BlockSpec((1, d, tf), lambda i, f: (layer, 0, f)),
                  pl.BlockSpec((1, tf, d), lambda i, f: (layer, f, 0))],
        out_specs=pl.BlockSpec((tm, d), lambda i, f: (i, 0)),
        out_shape=jax.ShapeDtypeStruct((m, d), F32),
        scratch_shapes=[pltpu.VMEM((tm, d), BF16)],
        compiler_params=_params("parallel", "arbitrary"),
        name="ffn",
    )(h, g, w1, w2)


def _ple_kernel(h_ref, g_ref, wg_ref, p_ref, wp_ref, gf_ref, o_ref, *, final_norm):
    x = h_ref[...]
    gate = _sigmoid(_dot(_rms(x, g_ref[...]).astype(BF16), wg_ref[0]))
    out = x + gate * _dot(p_ref[0].astype(BF16), wp_ref[0])
    if final_norm:
        out = _rms(out, gf_ref[...])
    o_ref[...] = out


def ple_gate(h, g, w_gate, p, w_p, layer, g_final, final_norm, tm):
    m, d = h.shape
    const = lambda i: (0, 0)
    slab = lambda i: (layer, 0, 0)
    return pl.pallas_call(
        functools.partial(_ple_kernel, final_norm=final_norm),
        grid=(m // tm,),
        in_specs=[pl.BlockSpec((tm, d), lambda i: (i, 0)),
                  pl.BlockSpec((1, d), const),
                  pl.BlockSpec((1,) + w_gate.shape[1:], slab),
                  pl.BlockSpec((1, tm, p.shape[2]), lambda i: (layer, i, 0)),
                  pl.BlockSpec((1,) + w_p.shape[1:], slab),
                  pl.BlockSpec((1, d), const)],
        out_specs=pl.BlockSpec((tm, d), lambda i: (i, 0)),
        out_shape=jax.ShapeDtypeStruct((m, d), F32),
        compiler_params=_params("parallel"),
        name="ple_gate",
    )(h, g, w_gate, p, w_p, g_final)


def _tile(n, target):
    t = min(n, target)
    assert n % t == 0, (n, t)
    return t


def _tiles(m, d, dff):
    return {
        "proj_m": _tile(m, 1024),
        "conv_c": 256,
        "merge_m": _tile(m, 1024), "merge_n": _tile(d, 1024),
        "row_m": _tile(m, 512),
        "ffn_m": _tile(m, 1024), "ffn_f": _tile(dff, 512),
    }


def kernel(x, p, w_in, conv_w, hg_lb_logits, hg_norm_g, att_rel_bias, w_branch, w_o,
           w_ff1, w_ff2, w_ple_in, w_ple_gate, g_mix, g_ff, g_ple, g_final):
    depth = w_in.shape[0]
    b, s, d = x.shape
    m = b * s
    assert s % ATT_GQ == 0 and s % HG_BLOCK == 0 and d % MIX_WIDTH == 0

    lb_sm = jax.nn.softmax(hg_lb_logits.astype(F32), axis=0)
    lb_all = jnp.cumsum(lb_sm, axis=0) - lb_sm[0]

    w = MIX_WIDTH
    hf_tile = COL_HI
    col_scale = jnp.ones((w_in.shape[2],), F32).at[(COL_AQ + 1) * w:(COL_AQ + 2) * w].set(
        HEAD_DIM ** -0.5 * LOG2_E)
    w_in_b = (w_in * col_scale).astype(BF16)
    w_branch_b, w_o_b = w_branch.astype(BF16), w_o.astype(BF16)
    w_ff1_b, w_ff2_b = w_ff1.astype(BF16), w_ff2.astype(BF16)
    w_gate_b, w_ple_b = w_ple_gate.astype(BF16), w_ple_in.astype(BF16)
    p2 = p.reshape(depth, m, p.shape[3])

    t = _tiles(m, d, w_ff1.shape[2])
    h = x.reshape(m, d)
    for l in range(depth):
        proj, z = in_proj(h, g_mix[l].reshape(1, d), w_in_b, l, hf_tile, tm=t["proj_m"], tn=w)
        proj3 = proj.reshape(b, s, proj.shape[1])
        y_conv = conv_branch(proj3, conv_w[l], tc=t["conv_c"])
        y_hg = hgrn_branch(proj3, z.reshape(b, s, w), lb_all[l], hg_norm_g[l], HG_HEADS_PER_STEP)
        y_att = attention_branch(proj3, att_rel_bias[l])
        merged = merge_branches(y_conv.reshape(m, w), y_hg.reshape(m, w), y_att.reshape(m, w),
                                w_branch_b, l, proj, tm=t["merge_m"], tn=t["merge_n"])
        h = out_proj(merged, w_o_b, l, h, tm=t["row_m"])
        h = ffn(h, g_ff[l].reshape(1, d), w_ff1_b, w_ff2_b, l, tm=t["ffn_m"], tf=t["ffn_f"])
        h = ple_gate(h, g_ple[l].reshape(1, d), w_gate_b, p2, w_ple_b, l, g_final.reshape(1, d),
                     final_norm=(l == depth - 1), tm=t["row_m"])
    return h.reshape(b, s, d)
```

```python
import functools

import numpy as np
import jax
import jax.numpy as jnp
from jax import lax
from jax.experimental import pallas as pl
from jax.experimental.pallas import tpu as pltpu

F32 = jnp.float32
BF16 = jnp.bfloat16

EPS = 1e-6
LB_FLOOR = 1e-30
MASK_VALUE = -1e30
LOG2_E = 1.4426950408889634

MIX_WIDTH = 1024
CONV_K = 3
HEADS = 8
HEAD_DIM = 128
CHUNK = 64
LEFT_CHUNKS = 8
MAX_REL_DIST = 256

HG_BLOCK = 128
HG_LEVELS = 7
ATT_GROUP_CHUNKS = 4
ATT_GQ = ATT_GROUP_CHUNKS * CHUNK
ATT_GK = (ATT_GROUP_CHUNKS + LEFT_CHUNKS) * CHUNK
HG_HEADS_PER_STEP = 8

VMEM_LIMIT_BYTES = 58 * 1024 * 1024

COL_CB, COL_CC, COL_CH, COL_HQ, COL_HI, COL_HG, COL_AQ, COL_AK, COL_AV = range(9)
N_MIX_BLOCKS = 9
REF_HF_TILE = 4
REF_AQ_TILE = 7
REF_FIRST_GATE_TILE = 10


def _mix_block(proj, block):
    return proj.shape[-1] // MIX_WIDTH - N_MIX_BLOCKS + block


def _params(*sem):
    return pltpu.CompilerParams(dimension_semantics=sem, vmem_limit_bytes=VMEM_LIMIT_BYTES)


def _rms(x, g):
    ms = jnp.mean(x * x, axis=-1, keepdims=True)
    return x * lax.rsqrt(ms + EPS) * g


def _dot(a, b):
    return jnp.dot(a, b, preferred_element_type=F32)


def _dot_nt(a, b):
    return lax.dot_general(a, b, (((1,), (1,)), ((), ())), preferred_element_type=F32)


def _dot_tn(a, b):
    return lax.dot_general(a, b, (((0,), (0,)), ((), ())), preferred_element_type=F32)


def _sigmoid(x):
    return 1.0 / (1.0 + jnp.exp(-x))


def _in_proj_kernel(x_ref, g_ref, w_ref, ob_ref, of_ref, *scratch, normalize):
    j = pl.program_id(1)
    if normalize:
        xn_ref, = scratch

        @pl.when(j == 0)
        def _():
            xn_ref[...] = _rms(x_ref[...], g_ref[...]).astype(BF16)
    else:
        xn_ref = x_ref

    @pl.when(j != REF_HF_TILE)
    def _():
        ob_ref[...] = _dot(xn_ref[...], w_ref[0]).astype(BF16)

    @pl.when(j == REF_HF_TILE)
    def _():
        of_ref[...] = _dot(xn_ref[...], w_ref[0])


def in_proj(x, g, w, layer, tm):
    m, d = x.shape
    n = w.shape[2]
    tn = MIX_WIDTH
    n_gate_tiles = n // tn - REF_FIRST_GATE_TILE
    normalize = x.dtype != BF16

    def bf16_block(j):
        mixer = n_gate_tiles + jnp.where(j < REF_HF_TILE, j, jnp.maximum(j - 1, REF_HF_TILE - 1))
        return jnp.where(j >= REF_FIRST_GATE_TILE, j - REF_FIRST_GATE_TILE, mixer)

    return pl.pallas_call(
        functools.partial(_in_proj_kernel, normalize=normalize),
        grid=(m // tm, n // tn),
        in_specs=[
            pl.BlockSpec((tm, d), lambda i, j: (i, 0), pipeline_mode=None if normalize else pl.Buffered(1)),
            pl.BlockSpec((1, d), lambda i, j: (0, 0)),
            pl.BlockSpec((1, d, tn), lambda i, j: (layer, 0, j)),
        ],
        out_specs=[
            pl.BlockSpec((tm, tn), lambda i, j: (i, bf16_block(j))),
            pl.BlockSpec((tm, tn), lambda i, j: (i, 0)),
        ],
        out_shape=[
            jax.ShapeDtypeStruct((m, n - tn), BF16),
            jax.ShapeDtypeStruct((m, tn), F32),
        ],
        scratch_shapes=[pltpu.VMEM((tm, d), BF16)] if normalize else [],
        compiler_params=_params("parallel", "arbitrary"),
        name="in_proj",
    )(x, g, w)


def _conv_kernel(cb_ref, cc_ref, ch_ref, w_ref, o_ref):
    u = cc_ref[0].astype(F32) * ch_ref[0].astype(F32)
    row = lax.broadcasted_iota(jnp.int32, u.shape, 0)
    u1 = jnp.where(row >= 1, pltpu.roll(u, 1, 0), 0.0)
    u2 = jnp.where(row >= 2, pltpu.roll(u, 2, 0), 0.0)
    w = w_ref[...]
    y = u2 * w[0:1, :] + u1 * w[1:2, :] + u * w[2:3, :]
    o_ref[0] = (cb_ref[0].astype(F32) * y).astype(BF16)


def conv_branch(proj, conv_w, tc):
    b, s, _ = proj.shape
    per = MIX_WIDTH // tc

    def col(block):
        return pl.BlockSpec((1, s, tc), lambda bi, c: (bi, 0, _mix_block(proj, block) * per + c))

    return pl.pallas_call(
        _conv_kernel,
        grid=(b, per),
        in_specs=[col(COL_CB), col(COL_CC), col(COL_CH),
                  pl.BlockSpec((CONV_K, tc), lambda bi, c: (0, c))],
        out_specs=pl.BlockSpec((1, s, tc), lambda bi, c: (bi, 0, c)),
        out_shape=jax.ShapeDtypeStruct((b, s, MIX_WIDTH), BF16),
        compiler_params=_params("parallel", "parallel"),
        name="conv_branch",
    )(proj, proj, proj, conv_w)


def _hgrn_level_tables():
    t = np.arange(HG_BLOCK)
    x = t[:, None] ^ t[None, :]
    lvl = np.where(t[:, None] > t[None, :], np.floor(np.log2(np.maximum(x, 1))).astype(np.int64) + 1, -1)
    lvl = np.where(t[:, None] == t[None, :], 0, lvl).astype(np.int32)
    tri = (t[:, None] >= t[None, :]).astype(np.float32)
    return lvl, tri


def _split_point_rows(g, level):
    size, half = 1 << level, 1 << (level - 1)
    n = g.shape[0]
    if size >= 8:
        parts = [jnp.broadcast_to(g[b * size + half - 1:b * size + half, :], (size, g.shape[1]))
                 for b in range(n // size)]
        return jnp.concatenate(parts, axis=0)
    sub = lax.broadcasted_iota(jnp.int32, (8, g.shape[1]), 0)
    parts = []
    for v in range(n // 8):
        acc = None
        for b in range(8 // size):
            r = 8 * v + b * size + half - 1
            cand = jnp.broadcast_to(g[r:r + 1, :], (8, g.shape[1]))
            acc = cand if acc is None else jnp.where(sub >= b * size, cand, acc)
        parts.append(acc)
    return jnp.concatenate(parts, axis=0)


def _query_or_key_rows(q, key, level):
    size, half = 1 << level, 1 << (level - 1)
    n = q.shape[0]
    if half >= 8:
        parts = []
        for b in range(n // size):
            parts += [key[b * size:b * size + half], q[b * size + half:(b + 1) * size]]
        return jnp.concatenate(parts, axis=0)
    row = lax.broadcasted_iota(jnp.int32, q.shape, 0)
    return jnp.where((row & half) != 0, q, key)


def _neg_abs(x):
    bits = lax.bitcast_convert_type(x, jnp.uint32) | jnp.uint32(0x80000000)
    return lax.bitcast_convert_type(bits, F32)


def _hgrn_head_block(q, v, gate, z, lb_floor, one_m_lb, ng, tri, lvl, st):
    L = HG_BLOCK
    e = jnp.exp(_neg_abs(z))
    inv = 1.0 / (1.0 + e)
    einv = e * inv
    pos = z >= 0
    f = lb_floor + one_m_lb * jnp.where(pos, inv, einv)
    key = one_m_lb * jnp.where(pos, einv, inv)
    lf2 = jnp.log2(f)
    hi = lf2.astype(BF16)
    lo = (lf2 - hi.astype(F32)).astype(BF16)
    c2 = _dot(tri, jnp.concatenate([hi, lo], axis=1))
    g2 = c2[:, :HEAD_DIM] + c2[:, HEAD_DIM:]

    p01 = _dot_nt(jnp.concatenate([q, q * f], axis=0).astype(BF16), key.astype(BF16))
    a = jnp.where(lvl == 0, p01[:L], 0.0)
    a = jnp.where(lvl == 1, p01[L:], a)
    for level in range(2, HG_LEVELS + 1):
        d = jnp.exp2(_neg_abs(g2 - _split_point_rows(g2, level)))
        x = (_query_or_key_rows(q, key, level) * d).astype(BF16)
        a = jnp.where(lvl == level, _dot_nt(x, x), a)

    g_tot = g2[L - 1:L, :]
    o = _dot(a.astype(BF16), v) + _dot_nt((q * jnp.exp2(g2)).astype(BF16), st.astype(BF16))
    kd = (key * jnp.exp2(g_tot - g2)).astype(BF16)
    st_new = st * jnp.exp2(g_tot) + _dot_tn(v, kd)
    y = _rms(o, ng) * (gate * _sigmoid(gate))
    return y, st_new


def _hgrn_kernel(q_ref, v_ref, gate_ref, z_ref, lb_ref, ng_ref, tri_ref, lvl_ref, o_ref, st_ref):
    L = HG_BLOCK
    n_blocks = q_ref.shape[1] // L
    n_heads = q_ref.shape[2] // HEAD_DIM
    lb = lb_ref[0]
    lb_floor = jnp.maximum(lb, LB_FLOOR)
    one_m_lb = 1.0 - lb
    ng = ng_ref[...]
    tri = tri_ref[...]
    lvl = lvl_ref[...]
    st_ref[...] = jnp.zeros(st_ref.shape, F32)

    def body(n, carry):
        rows = pl.ds(pl.multiple_of(n * L, L), L)
        for hh in range(n_heads):
            cols = slice(hh * HEAD_DIM, (hh + 1) * HEAD_DIM)
            y, st_new = _hgrn_head_block(
                q_ref[0, rows, cols].astype(F32), v_ref[0, rows, cols], gate_ref[0, rows, cols].astype(F32),
                z_ref[0, rows, cols], lb_floor[:, cols], one_m_lb[:, cols], ng, tri, lvl, st_ref[hh])
            st_ref[hh] = st_new
            o_ref[0, rows, cols] = y.astype(BF16)
        return carry

    lax.fori_loop(0, n_blocks, body, 0)


def hgrn_branch(proj, z, lb, norm_g, heads_per_step):
    b, s, _ = proj.shape
    lvl, tri = _hgrn_level_tables()
    width = heads_per_step * HEAD_DIM
    per = MIX_WIDTH // width

    def col(block):
        return pl.BlockSpec((1, s, width), lambda bi, h: (bi, 0, _mix_block(proj, block) * per + h))

    const = lambda bi, h: (0, 0)
    return pl.pallas_call(
        _hgrn_kernel,
        grid=(b, per),
        in_specs=[col(COL_HQ), col(COL_HI), col(COL_HG),
                  pl.BlockSpec((1, s, width), lambda bi, h: (bi, 0, h)),
                  pl.BlockSpec((1, 1, width), lambda bi, h: (h, 0, 0)),
                  pl.BlockSpec((1, HEAD_DIM), const),
                  pl.BlockSpec((HG_BLOCK, HG_BLOCK), const),
                  pl.BlockSpec((HG_BLOCK, HG_BLOCK), const)],
        out_specs=pl.BlockSpec((1, s, width), lambda bi, h: (bi, 0, h)),
        out_shape=jax.ShapeDtypeStruct((b, s, MIX_WIDTH), BF16),
        scratch_shapes=[pltpu.VMEM((heads_per_step, HEAD_DIM, HEAD_DIM), F32)],
        compiler_params=_params("parallel", "parallel"),
        name="hgrn_branch",
    )(proj, proj, proj, z, lb.reshape(per, 1, width), norm_g.reshape(1, HEAD_DIM),
      jnp.asarray(tri, BF16), jnp.asarray(lvl))


def _att_group_bias(rel_table):
    heads = rel_table.shape[0]
    table = rel_table.astype(F32) * LOG2_E
    band = np.arange(ATT_GK)[None, :] // CHUNK - np.arange(ATT_GQ)[:, None] // CHUNK
    valid = (band >= 0) & (band <= LEFT_CHUNKS)
    lo, hi = LEFT_CHUNKS * CHUNK - (ATT_GK - 1), LEFT_CHUNKS * CHUNK + ATT_GQ - 1
    assert -MAX_REL_DIST <= lo and hi >= MAX_REL_DIST
    period = hi - lo + 2
    wr = jnp.concatenate([jnp.broadcast_to(table[:, 2 * MAX_REL_DIST:], (heads, hi - MAX_REL_DIST)),
                          jnp.flip(table[:, lo + MAX_REL_DIST:], axis=1),
                          jnp.zeros((heads, 1), F32)], axis=1)
    m = jnp.tile(wr, (1, ATT_GQ))[:, :ATT_GQ * (period - 1)].reshape(heads, ATT_GQ, period - 1)
    bias = m[:, :, ATT_GQ - 1:ATT_GQ - 1 + ATT_GK]
    return jnp.where(jnp.asarray(valid)[None], bias, MASK_VALUE)


def _att_kernel(q_ref, k_ref, v_ref, bias_ref, o_ref):
    n_groups = q_ref.shape[1] // ATT_GQ

    for g in range(n_groups):
        q0 = g * ATT_GQ
        k0 = max(0, q0 + ATT_GQ - ATT_GK)
        nk = q0 + ATT_GQ - k0
        q = q_ref[0, q0:q0 + ATT_GQ, :]
        k = k_ref[0, k0:k0 + nk, :]
        v = v_ref[0, k0:k0 + nk, :]
        s = _dot_nt(q, k) + bias_ref[0, :, ATT_GK - nk:]
        m = jnp.max(s, axis=-1, keepdims=True)
        p = jnp.exp2(s - m).astype(BF16)
        ov = _dot(p, jnp.concatenate([v, jnp.ones_like(v)], axis=1))
        o = ov[:, :HEAD_DIM] / ov[:, HEAD_DIM:]
        o_ref[0, q0:q0 + ATT_GQ, :] = o.astype(BF16)


def attention_branch(proj, rel_table):
    b, s, _ = proj.shape
    bias = _att_group_bias(rel_table)

    def col(block):
        return pl.BlockSpec((1, s, HEAD_DIM), lambda h, bi: (bi, 0, _mix_block(proj, block) * HEADS + h))

    return pl.pallas_call(
        _att_kernel,
        grid=(HEADS, b),
        in_specs=[col(COL_AQ), col(COL_AK), col(COL_AV),
                  pl.BlockSpec((1, ATT_GQ, ATT_GK), lambda h, bi: (h, 0, 0))],
        out_specs=pl.BlockSpec((1, s, HEAD_DIM), lambda h, bi: (bi, 0, h)),
        out_shape=jax.ShapeDtypeStruct((b, s, MIX_WIDTH), BF16),
        compiler_params=_params("parallel", "parallel"),
        name="attention_branch",
    )(proj, proj, proj, bias)


def _mix_out_kernel(yc_ref, yh_ref, ya_ref, gc_ref, gh_ref, ga_ref, h_ref, wb_ref, wo_ref, gn_ref,
                    h1_ref, hn_ref):
    merged = _sigmoid(gc_ref[...].astype(F32)) * _dot(yc_ref[...], wb_ref[0, 0])
    merged += _sigmoid(gh_ref[...].astype(F32)) * _dot(yh_ref[...], wb_ref[0, 1])
    merged += _sigmoid(ga_ref[...].astype(F32)) * _dot(ya_ref[...], wb_ref[0, 2])
    h1 = h_ref[...] + _dot(merged.astype(BF16), wo_ref[0])
    h1_ref[...] = h1
    hn_ref[...] = _rms(h1, gn_ref[...]).astype(BF16)


def mix_out(y_conv, y_hg, y_att, proj, h, w_branch, w_o, layer, g_next, tm):
    m, d = h.shape
    kdim = y_conv.shape[1]
    rows = lambda width: pl.BlockSpec((tm, width), lambda i: (i, 0))
    gate = lambda branch: pl.BlockSpec((tm, d), lambda i: (i, branch))
    resident = pl.Buffered(1)
    return pl.pallas_call(
        _mix_out_kernel,
        grid=(m // tm,),
        in_specs=[rows(kdim), rows(kdim), rows(kdim), gate(0), gate(1), gate(2), rows(d),
                  pl.BlockSpec((1,) + w_branch.shape[1:], lambda i: (layer, 0, 0, 0), pipeline_mode=resident),
                  pl.BlockSpec((1,) + w_o.shape[1:], lambda i: (layer, 0, 0), pipeline_mode=resident),
                  pl.BlockSpec((1, d), lambda i: (0, 0))],
        out_specs=[rows(d), rows(d)],
        out_shape=[jax.ShapeDtypeStruct((m, d), F32), jax.ShapeDtypeStruct((m, d), BF16)],
        compiler_params=_params("parallel"),
        name="mix_out",
    )(y_conv, y_hg, y_att, proj, proj, proj, h, w_branch, w_o, g_next)


def _ffn_kernel(hn_ref, w1_ref, w2_ref, o_ref, acc_ref, *, n_steps):
    f = pl.program_id(1)
    last = n_steps - 1

    def part():
        hid = jnp.square(jnp.maximum(_dot(hn_ref[...], w1_ref[0]), 0.0))
        return _dot(hid.astype(BF16), w2_ref[0])

    if n_steps == 1:
        o_ref[...] = part().astype(BF16)
        return

    @pl.when(f == 0)
    def _():
        acc_ref[...] = part()

    @pl.when((f > 0) & (f < last))
    def _():
        acc_ref[...] += part()

    @pl.when(f == last)
    def _():
        o_ref[...] = (acc_ref[...] + part()).astype(BF16)


def ffn(hn, w1, w2, layer, tm, tf):
    m, d = hn.shape
    dff = w1.shape[2]
    return pl.pallas_call(
        functools.partial(_ffn_kernel, n_steps=dff // tf),
        grid=(m // tm, dff // tf),
        in_specs=[pl.BlockSpec((tm, d), lambda i, f: (i, 0)),
                  pl.BlockSpec((1, d, tf), lambda i, f: (layer, 0, f)),
                  pl.BlockSpec((1, tf, d), lambda i, f: (layer, f, 0))],
        out_specs=pl.BlockSpec((tm, d), lambda i, f: (i, 0)),
        out_shape=jax.ShapeDtypeStruct((m, d), BF16),
        scratch_shapes=[pltpu.VMEM((tm, d), F32)],
        compiler_params=_params("parallel", "arbitrary"),
        name="ffn",
    )(hn, w1, w2)


def _ple_kernel(h_ref, y_ref, g_ref, wg_ref, p_ref, wp_ref, gn_ref, *out_refs, last):
    x = h_ref[...] + y_ref[...].astype(F32)
    gate = _sigmoid(_dot(_rms(x, g_ref[...]).astype(BF16), wg_ref[0]))
    out = x + gate * _dot(p_ref[0].astype(BF16), wp_ref[0])
    if last:
        out_refs[0][...] = _rms(out, gn_ref[...])
    else:
        out_refs[0][...] = out
        out_refs[1][...] = _rms(out, gn_ref[...]).astype(BF16)


def ple_gate(h, y, g, w_gate, p, w_p, layer, g_next, last, tm):
    m, d = h.shape
    const = lambda i: (0, 0)
    slab = lambda i: (layer, 0, 0)
    rows = pl.BlockSpec((tm, d), lambda i: (i, 0))
    resident = pl.Buffered(1)
    if last:
        out_specs, out_shape = [rows], [jax.ShapeDtypeStruct((m, d), F32)]
    else:
        out_specs = [rows, rows]
        out_shape = [jax.ShapeDtypeStruct((m, d), F32), jax.ShapeDtypeStruct((m, d), BF16)]
    return pl.pallas_call(
        functools.partial(_ple_kernel, last=last),
        grid=(m // tm,),
        in_specs=[rows, rows,
                  pl.BlockSpec((1, d), const),
                  pl.BlockSpec((1,) + w_gate.shape[1:], slab, pipeline_mode=resident),
                  pl.BlockSpec((1, tm, p.shape[2]), lambda i: (layer, i, 0)),
                  pl.BlockSpec((1,) + w_p.shape[1:], slab, pipeline_mode=resident),
                  pl.BlockSpec((1, d), const)],
        out_specs=out_specs,
        out_shape=out_shape,
        compiler_params=_params("parallel"),
        name="ple_gate",
    )(h, y, g, w_gate, p, w_p, g_next)


def _tile(n, target):
    t = min(n, target)
    assert n % t == 0, (n, t)
    return t


def _tiles(m, d, dff):
    return {
        "proj_m_f32": _tile(m, 1024),
        "proj_m_bf16": _tile(m, 2048),
        "conv_c": 256,
        "mix_m": _tile(m, 256),
        "ffn_m": _tile(m, 1024), "ffn_f": _tile(dff, 1024),
        "ple_m": _tile(m, 512),
    }


def kernel(x, p, w_in, conv_w, hg_lb_logits, hg_norm_g, att_rel_bias, w_branch, w_o,
           w_ff1, w_ff2, w_ple_in, w_ple_gate, g_mix, g_ff, g_ple, g_final):
    depth = w_in.shape[0]
    b, s, d = x.shape
    m = b * s
    w = MIX_WIDTH
    assert s % ATT_GQ == 0 and s % HG_BLOCK == 0 and d % w == 0
    assert w_in.shape[2] == (REF_FIRST_GATE_TILE + 3 * d // w) * w

    lb_sm = jax.nn.softmax(hg_lb_logits.astype(F32), axis=0)
    lb_all = jnp.cumsum(lb_sm, axis=0) - lb_sm[0]

    col_scale = jnp.ones((w_in.shape[2],), F32).at[REF_AQ_TILE * w:(REF_AQ_TILE + 1) * w].set(
        HEAD_DIM ** -0.5 * LOG2_E)
    w_in_b = (w_in * col_scale).astype(BF16)
    w_branch_b, w_o_b = w_branch.astype(BF16), w_o.astype(BF16)
    w_ff1_b, w_ff2_b = w_ff1.astype(BF16), w_ff2.astype(BF16)
    w_gate_b, w_ple_b = w_ple_gate.astype(BF16), w_ple_in.astype(BF16)
    p2 = p.reshape(depth, m, p.shape[3])

    t = _tiles(m, d, w_ff1.shape[2])
    h = x.reshape(m, d)
    xn = h
    for l in range(depth):
        last = l == depth - 1
        proj, z = in_proj(xn, g_mix[l].reshape(1, d), w_in_b, l,
                          tm=t["proj_m_bf16"] if xn.dtype == BF16 else t["proj_m_f32"])
        proj3 = proj.reshape(b, s, proj.shape[1])
        y_conv = conv_branch(proj3, conv_w[l], tc=t["conv_c"])
        y_hg = hgrn_branch(proj3, z.reshape(b, s, w), lb_all[l], hg_norm_g[l], HG_HEADS_PER_STEP)
        y_att = attention_branch(proj3, att_rel_bias[l])
        h1, hn = mix_out(y_conv.reshape(m, w), y_hg.reshape(m, w), y_att.reshape(m, w), proj, h,
                         w_branch_b, w_o_b, l, g_ff[l].reshape(1, d), tm=t["mix_m"])
        y = ffn(hn, w_ff1_b, w_ff2_b, l, tm=t["ffn_m"], tf=t["ffn_f"])
        g_next = (g_final if last else g_mix[l + 1]).reshape(1, d)
        outs = ple_gate(h1, y, g_ple[l].reshape(1, d), w_gate_b, p2, w_ple_b, l, g_next, last, tm=t["ple_m"])
        if last:
            h, = outs
        else:
            h, xn = outs
    return h.reshape(b, s, d)
```

```python
import functools

import numpy as np
import jax
import jax.numpy as jnp
from jax import lax
from jax.experimental import pallas as pl
from jax.experimental.pallas import tpu as pltpu

F32 = jnp.float32
BF16 = jnp.bfloat16

EPS = 1e-6
LB_FLOOR = 1e-30
MASK_VALUE = -1e30
LOG2_E = 1.4426950408889634

MIX_WIDTH = 1024
CONV_K = 3
HEADS = 8
HEAD_DIM = 128
CHUNK = 64
LEFT_CHUNKS = 8
MAX_REL_DIST = 256

HG_BLOCK = 128
HG_LEVELS = 7
ATT_GROUP_CHUNKS = 4
ATT_GQ = ATT_GROUP_CHUNKS * CHUNK
ATT_GK = (ATT_GROUP_CHUNKS + LEFT_CHUNKS) * CHUNK
HG_HEADS_PER_STEP = 8

VMEM_LIMIT_BYTES = 58 * 1024 * 1024

COL_CB, COL_CC, COL_CH, COL_HQ, COL_HI, COL_HG, COL_AQ, COL_AK, COL_AV = range(9)
N_MIX_BLOCKS = 9
REF_HF_TILE = 4
REF_AQ_TILE = 7
REF_FIRST_GATE_TILE = 10


def _mix_block(proj, block):
    return proj.shape[-1] // MIX_WIDTH - N_MIX_BLOCKS + block


def _params(*sem):
    return pltpu.CompilerParams(dimension_semantics=sem, vmem_limit_bytes=VMEM_LIMIT_BYTES)


def _rms(x, g):
    ms = jnp.mean(x * x, axis=-1, keepdims=True)
    return x * lax.rsqrt(ms + EPS) * g


def _dot(a, b):
    return jnp.dot(a, b, preferred_element_type=F32)


def _dot_nt(a, b):
    return lax.dot_general(a, b, (((1,), (1,)), ((), ())), preferred_element_type=F32)


def _dot_tn(a, b):
    return lax.dot_general(a, b, (((0,), (0,)), ((), ())), preferred_element_type=F32)


def _sigmoid(x):
    return 1.0 / (1.0 + jnp.exp(-x))


def _in_proj_kernel(x_ref, g_ref, w_ref, ob_ref, of_ref, *scratch, normalize):
    j = pl.program_id(1)
    if normalize:
        xn_ref, = scratch

        @pl.when(j == 0)
        def _():
            xn_ref[...] = _rms(x_ref[...], g_ref[...]).astype(BF16)
    else:
        xn_ref = x_ref

    @pl.when(j != REF_HF_TILE)
    def _():
        ob_ref[...] = _dot(xn_ref[...], w_ref[0]).astype(BF16)

    @pl.when(j == REF_HF_TILE)
    def _():
        of_ref[...] = _dot(xn_ref[...], w_ref[0])


def in_proj(x, g, w, layer, tm):
    m, d = x.shape
    n = w.shape[2]
    tn = MIX_WIDTH
    n_gate_tiles = n // tn - REF_FIRST_GATE_TILE
    normalize = x.dtype != BF16

    def bf16_block(j):
        mixer = n_gate_tiles + jnp.where(j < REF_HF_TILE, j, jnp.maximum(j - 1, REF_HF_TILE - 1))
        return jnp.where(j >= REF_FIRST_GATE_TILE, j - REF_FIRST_GATE_TILE, mixer)

    return pl.pallas_call(
        functools.partial(_in_proj_kernel, normalize=normalize),
        grid=(m // tm, n // tn),
        in_specs=[
            pl.BlockSpec((tm, d), lambda i, j: (i, 0), pipeline_mode=None if normalize else pl.Buffered(1)),
            pl.BlockSpec((1, d), lambda i, j: (0, 0)),
            pl.BlockSpec((1, d, tn), lambda i, j: (layer, 0, j)),
        ],
        out_specs=[
            pl.BlockSpec((tm, tn), lambda i, j: (i, bf16_block(j))),
            pl.BlockSpec((tm, tn), lambda i, j: (i, 0)),
        ],
        out_shape=[
            jax.ShapeDtypeStruct((m, n - tn), BF16),
            jax.ShapeDtypeStruct((m, tn), F32),
        ],
        scratch_shapes=[pltpu.VMEM((tm, d), BF16)] if normalize else [],
        compiler_params=_params("parallel", "arbitrary"),
        name="in_proj",
    )(x, g, w)


def _hgrn_level_tables():
    t = np.arange(HG_BLOCK)
    x = t[:, None] ^ t[None, :]
    lvl = np.where(t[:, None] > t[None, :], np.floor(np.log2(np.maximum(x, 1))).astype(np.int64) + 1, -1)
    lvl = np.where(t[:, None] == t[None, :], 0, lvl).astype(np.int32)
    tri = (t[:, None] >= t[None, :]).astype(np.float32)
    return lvl, tri


def _split_point_rows(g, level):
    size, half = 1 << level, 1 << (level - 1)
    n = g.shape[0]
    if size >= 8:
        parts = [jnp.broadcast_to(g[b * size + half - 1:b * size + half, :], (size, g.shape[1]))
                 for b in range(n // size)]
        return jnp.concatenate(parts, axis=0)
    sub = lax.broadcasted_iota(jnp.int32, (8, g.shape[1]), 0)
    parts = []
    for v in range(n // 8):
        acc = None
        for b in range(8 // size):
            r = 8 * v + b * size + half - 1
            cand = jnp.broadcast_to(g[r:r + 1, :], (8, g.shape[1]))
            acc = cand if acc is None else jnp.where(sub >= b * size, cand, acc)
        parts.append(acc)
    return jnp.concatenate(parts, axis=0)


def _query_or_key_rows(q, key, level):
    size, half = 1 << level, 1 << (level - 1)
    n = q.shape[0]
    if half >= 8:
        parts = []
        for b in range(n // size):
            parts += [key[b * size:b * size + half], q[b * size + half:(b + 1) * size]]
        return jnp.concatenate(parts, axis=0)
    row = lax.broadcasted_iota(jnp.int32, q.shape, 0)
    return jnp.where((row & half) != 0, q, key)


def _neg_abs(x):
    bits = lax.bitcast_convert_type(x, jnp.uint32) | jnp.uint32(0x80000000)
    return lax.bitcast_convert_type(bits, F32)


def _hgrn_head_block(q, v, z, lb_floor, one_m_lb, tri, lvl, st):
    L = HG_BLOCK
    e = jnp.exp(_neg_abs(z))
    inv = 1.0 / (1.0 + e)
    einv = e * inv
    pos = z >= 0
    f = lb_floor + one_m_lb * jnp.where(pos, inv, einv)
    key = one_m_lb * jnp.where(pos, einv, inv)
    lf2 = jnp.log2(f)
    hi = lf2.astype(BF16)
    lo = (lf2 - hi.astype(F32)).astype(BF16)
    c2 = _dot(tri, jnp.concatenate([hi, lo], axis=1))
    g2 = c2[:, :HEAD_DIM] + c2[:, HEAD_DIM:]

    p01 = _dot_nt(jnp.concatenate([q, q * f], axis=0).astype(BF16), key.astype(BF16))
    a = jnp.where(lvl == 0, p01[:L], 0.0)
    a = jnp.where(lvl == 1, p01[L:], a)
    for level in range(2, HG_LEVELS + 1):
        d = jnp.exp2(_neg_abs(g2 - _split_point_rows(g2, level)))
        x = (_query_or_key_rows(q, key, level) * d).astype(BF16)
        a = jnp.where(lvl == level, _dot_nt(x, x), a)

    g_tot = g2[L - 1:L, :]
    o = _dot(a.astype(BF16), v) + _dot_nt((q * jnp.exp2(g2)).astype(BF16), st.astype(BF16))
    kd = (key * jnp.exp2(g_tot - g2)).astype(BF16)
    st_new = st * jnp.exp2(g_tot) + _dot_tn(v, kd)
    return o, st_new


def _hgrn_kernel(q_ref, v_ref, z_ref, lb_ref, tri_ref, lvl_ref, o_ref, st_ref):
    L = HG_BLOCK
    n_blocks = q_ref.shape[1] // L
    n_heads = q_ref.shape[2] // HEAD_DIM
    lb = lb_ref[0]
    lb_floor = jnp.maximum(lb, LB_FLOOR)
    one_m_lb = 1.0 - lb
    tri = tri_ref[...]
    lvl = lvl_ref[...]
    st_ref[...] = jnp.zeros(st_ref.shape, F32)

    def body(n, carry):
        rows = pl.ds(pl.multiple_of(n * L, L), L)
        for hh in range(n_heads):
            cols = slice(hh * HEAD_DIM, (hh + 1) * HEAD_DIM)
            o, st_new = _hgrn_head_block(
                q_ref[0, rows, cols].astype(F32), v_ref[0, rows, cols], z_ref[0, rows, cols],
                lb_floor[:, cols], one_m_lb[:, cols], tri, lvl, st_ref[hh])
            st_ref[hh] = st_new
            o_ref[0, rows, cols] = o.astype(BF16)
        return carry

    lax.fori_loop(0, n_blocks, body, 0)


def hgrn_branch(proj, z, lb, heads_per_step):
    b, s, _ = proj.shape
    lvl, tri = _hgrn_level_tables()
    width = heads_per_step * HEAD_DIM
    per = MIX_WIDTH // width

    def col(block):
        return pl.BlockSpec((1, s, width), lambda bi, h: (bi, 0, _mix_block(proj, block) * per + h))

    const = lambda bi, h: (0, 0)
    return pl.pallas_call(
        _hgrn_kernel,
        grid=(b, per),
        in_specs=[col(COL_HQ), col(COL_HI),
                  pl.BlockSpec((1, s, width), lambda bi, h: (bi, 0, h)),
                  pl.BlockSpec((1, 1, width), lambda bi, h: (h, 0, 0)),
                  pl.BlockSpec((HG_BLOCK, HG_BLOCK), const),
                  pl.BlockSpec((HG_BLOCK, HG_BLOCK), const)],
        out_specs=pl.BlockSpec((1, s, width), lambda bi, h: (bi, 0, h)),
        out_shape=jax.ShapeDtypeStruct((b, s, MIX_WIDTH), BF16),
        scratch_shapes=[pltpu.VMEM((heads_per_step, HEAD_DIM, HEAD_DIM), F32)],
        compiler_params=_params("parallel", "parallel"),
        name="hgrn_branch",
    )(proj, proj, z, lb.reshape(per, 1, width), jnp.asarray(tri, BF16), jnp.asarray(lvl))


def _att_group_bias(rel_table):
    heads = rel_table.shape[0]
    table = rel_table.astype(F32) * LOG2_E
    band = np.arange(ATT_GK)[None, :] // CHUNK - np.arange(ATT_GQ)[:, None] // CHUNK
    valid = (band >= 0) & (band <= LEFT_CHUNKS)
    lo, hi = LEFT_CHUNKS * CHUNK - (ATT_GK - 1), LEFT_CHUNKS * CHUNK + ATT_GQ - 1
    assert -MAX_REL_DIST <= lo and hi >= MAX_REL_DIST
    period = hi - lo + 2
    wr = jnp.concatenate([jnp.broadcast_to(table[:, 2 * MAX_REL_DIST:], (heads, hi - MAX_REL_DIST)),
                          jnp.flip(table[:, lo + MAX_REL_DIST:], axis=1),
                          jnp.zeros((heads, 1), F32)], axis=1)
    m = jnp.tile(wr, (1, ATT_GQ))[:, :ATT_GQ * (period - 1)].reshape(heads, ATT_GQ, period - 1)
    bias = m[:, :, ATT_GQ - 1:ATT_GQ - 1 + ATT_GK]
    return jnp.where(jnp.asarray(valid)[None], bias, MASK_VALUE)


def _att_kernel(q_ref, k_ref, v_ref, bias_ref, o_ref):
    n_groups = q_ref.shape[1] // ATT_GQ

    for g in range(n_groups):
        q0 = g * ATT_GQ
        k0 = max(0, q0 + ATT_GQ - ATT_GK)
        nk = q0 + ATT_GQ - k0
        q = q_ref[0, q0:q0 + ATT_GQ, :]
        k = k_ref[0, k0:k0 + nk, :]
        v = v_ref[0, k0:k0 + nk, :]
        s = _dot_nt(q, k) + bias_ref[0, :, ATT_GK - nk:]
        m = jnp.max(s, axis=-1, keepdims=True)
        p = jnp.exp2(s - m).astype(BF16)
        ov = _dot(p, jnp.concatenate([v, jnp.ones_like(v)], axis=1))
        o = ov[:, :HEAD_DIM] / ov[:, HEAD_DIM:]
        o_ref[0, q0:q0 + ATT_GQ, :] = o.astype(BF16)


def attention_branch(proj, rel_table):
    b, s, _ = proj.shape
    bias = _att_group_bias(rel_table)

    def col(block):
        return pl.BlockSpec((1, s, HEAD_DIM), lambda h, bi: (bi, 0, _mix_block(proj, block) * HEADS + h))

    return pl.pallas_call(
        _att_kernel,
        grid=(HEADS, b),
        in_specs=[col(COL_AQ), col(COL_AK), col(COL_AV),
                  pl.BlockSpec((1, ATT_GQ, ATT_GK), lambda h, bi: (h, 0, 0))],
        out_specs=pl.BlockSpec((1, s, HEAD_DIM), lambda h, bi: (bi, 0, h)),
        out_shape=jax.ShapeDtypeStruct((b, s, MIX_WIDTH), BF16),
        compiler_params=_params("parallel", "parallel"),
        name="attention_branch",
    )(proj, proj, proj, bias)


CONV_HALO = 8


def _conv_branch_tile(cb, cc, ch, cc_prev, ch_prev, w, first_in_seq):
    u = cc.astype(F32) * ch.astype(F32)
    prev = jnp.where(first_in_seq, 0.0, cc_prev.astype(F32) * ch_prev.astype(F32))
    before1 = jnp.broadcast_to(prev[CONV_HALO - 1:CONV_HALO, :], u.shape)
    before2 = jnp.broadcast_to(prev[CONV_HALO - 2:CONV_HALO - 1, :], u.shape)
    row = lax.broadcasted_iota(jnp.int32, u.shape, 0)
    u1 = jnp.where(row >= 1, pltpu.roll(u, 1, 0), before1)
    u2 = jnp.where(row >= 2, pltpu.roll(u, 2, 0), jnp.where(row == 1, before1, before2))
    return cb.astype(F32) * (u2 * w[0:1, :] + u1 * w[1:2, :] + u * w[2:3, :])


def _hgrn_norm_gate(o, gate, ng):
    parts = []
    for hh in range(o.shape[1] // HEAD_DIM):
        cols = slice(hh * HEAD_DIM, (hh + 1) * HEAD_DIM)
        g = gate[:, cols]
        parts.append(_rms(o[:, cols], ng) * (g * _sigmoid(g)))
    return jnp.concatenate(parts, axis=1)


def _mix_out_kernel(cb_ref, cc_ref, ch_ref, ccp_ref, chp_ref, cw_ref, oh_ref, hg_ref, ng_ref, ya_ref,
                    gc_ref, gh_ref, ga_ref, h_ref, wb_ref, wo_ref, gn_ref, h1_ref, hn_ref, *, tiles_per_seq):
    first_in_seq = pl.program_id(0) % tiles_per_seq == 0
    y_conv = _conv_branch_tile(cb_ref[...], cc_ref[...], ch_ref[...], ccp_ref[...], chp_ref[...],
                               cw_ref[...], first_in_seq)
    y_hg = _hgrn_norm_gate(oh_ref[...].astype(F32), hg_ref[...].astype(F32), ng_ref[...])
    merged = _sigmoid(gc_ref[...].astype(F32)) * _dot(y_conv.astype(BF16), wb_ref[0, 0])
    merged += _sigmoid(gh_ref[...].astype(F32)) * _dot(y_hg.astype(BF16), wb_ref[0, 1])
    merged += _sigmoid(ga_ref[...].astype(F32)) * _dot(ya_ref[...], wb_ref[0, 2])
    h1 = h_ref[...] + _dot(merged.astype(BF16), wo_ref[0])
    h1_ref[...] = h1
    hn_ref[...] = _rms(h1, gn_ref[...]).astype(BF16)


def mix_out(proj, o_hg, y_att, h, conv_w, norm_g, w_branch, w_o, layer, g_next, seq_len, tm):
    m, d = h.shape
    w = MIX_WIDTH
    assert seq_len % tm == 0 and tm % CONV_HALO == 0
    rows = lambda width: pl.BlockSpec((tm, width), lambda i: (i, 0))
    mix = lambda block: pl.BlockSpec((tm, w), lambda i: (i, _mix_block(proj, block)))
    halo = lambda block: pl.BlockSpec(
        (CONV_HALO, w), lambda i: (jnp.maximum(i * (tm // CONV_HALO) - 1, 0), _mix_block(proj, block)))
    gate = lambda branch: pl.BlockSpec((tm, d), lambda i: (i, branch))
    const = lambda i: (0, 0)
    resident = pl.Buffered(1)
    return pl.pallas_call(
        functools.partial(_mix_out_kernel, tiles_per_seq=seq_len // tm),
        grid=(m // tm,),
        in_specs=[mix(COL_CB), mix(COL_CC), mix(COL_CH), halo(COL_CC), halo(COL_CH),
                  pl.BlockSpec((CONV_K, w), const),
                  rows(w), mix(COL_HG), pl.BlockSpec((1, HEAD_DIM), const), rows(w),
                  gate(0), gate(1), gate(2), rows(d),
                  pl.BlockSpec((1,) + w_branch.shape[1:], lambda i: (layer, 0, 0, 0), pipeline_mode=resident),
                  pl.BlockSpec((1,) + w_o.shape[1:], lambda i: (layer, 0, 0), pipeline_mode=resident),
                  pl.BlockSpec((1, d), const)],
        out_specs=[rows(d), rows(d)],
        out_shape=[jax.ShapeDtypeStruct((m, d), F32), jax.ShapeDtypeStruct((m, d), BF16)],
        compiler_params=_params("parallel"),
        name="mix_out",
    )(proj, proj, proj, proj, proj, conv_w, o_hg, proj, norm_g.reshape(1, HEAD_DIM), y_att,
      proj, proj, proj, h, w_branch, w_o, g_next)


def _ffn_kernel(hn_ref, w1_ref, w2_ref, o_ref, acc_ref, *, n_steps):
    f = pl.program_id(1)
    last = n_steps - 1

    def part():
        hid = jnp.square(jnp.maximum(_dot(hn_ref[...], w1_ref[0]), 0.0))
        return _dot(hid.astype(BF16), w2_ref[0])

    if n_steps == 1:
        o_ref[...] = part().astype(BF16)
        return

    @pl.when(f == 0)
    def _():
        acc_ref[...] = part()

    @pl.when((f > 0) & (f < last))
    def _():
        acc_ref[...] += part()

    @pl.when(f == last)
    def _():
        o_ref[...] = (acc_ref[...] + part()).astype(BF16)


def ffn(hn, w1, w2, layer, tm, tf):
    m, d = hn.shape
    dff = w1.shape[2]
    return pl.pallas_call(
        functools.partial(_ffn_kernel, n_steps=dff // tf),
        grid=(m // tm, dff // tf),
        in_specs=[pl.BlockSpec((tm, d), lambda i, f: (i, 0)),
                  pl.BlockSpec((1, d, tf), lambda i, f: (layer, 0, f)),
                  pl.BlockSpec((1, tf, d), lambda i, f: (layer, f, 0))],
        out_specs=pl.BlockSpec((tm, d), lambda i, f: (i, 0)),
        out_shape=jax.ShapeDtypeStruct((m, d), BF16),
        scratch_shapes=[pltpu.VMEM((tm, d), F32)],
        compiler_params=_params("parallel", "arbitrary"),
        name="ffn",
    )(hn, w1, w2)


def _ple_kernel(h_ref, y_ref, g_ref, wg_ref, p_ref, wp_ref, gn_ref, *out_refs, last):
    x = h_ref[...] + y_ref[...].astype(F32)
    gate = _sigmoid(_dot(_rms(x, g_ref[...]).astype(BF16), wg_ref[0]))
    out = x + gate * _dot(p_ref[0].astype(BF16), wp_ref[0])
    if last:
        out_refs[0][...] = _rms(out, gn_ref[...])
    else:
        out_refs[0][...] = out
        out_refs[1][...] = _rms(out, gn_ref[...]).astype(BF16)


def ple_gate(h, y, g, w_gate, p, w_p, layer, g_next, last, tm):
    m, d = h.shape
    const = lambda i: (0, 0)
    slab = lambda i: (layer, 0, 0)
    rows = pl.BlockSpec((tm, d), lambda i: (i, 0))
    resident = pl.Buffered(1)
    if last:
        out_specs, out_shape = [rows], [jax.ShapeDtypeStruct((m, d), F32)]
    else:
        out_specs = [rows, rows]
        out_shape = [jax.ShapeDtypeStruct((m, d), F32), jax.ShapeDtypeStruct((m, d), BF16)]
    return pl.pallas_call(
        functools.partial(_ple_kernel, last=last),
        grid=(m // tm,),
        in_specs=[rows, rows,
                  pl.BlockSpec((1, d), const),
                  pl.BlockSpec((1,) + w_gate.shape[1:], slab, pipeline_mode=resident),
                  pl.BlockSpec((1, tm, p.shape[2]), lambda i: (layer, i, 0)),
                  pl.BlockSpec((1,) + w_p.shape[1:], slab, pipeline_mode=resident),
                  pl.BlockSpec((1, d), const)],
        out_specs=out_specs,
        out_shape=out_shape,
        compiler_params=_params("parallel"),
        name="ple_gate",
    )(h, y, g, w_gate, p, w_p, g_next)


def _tile(n, target):
    t = min(n, target)
    assert n % t == 0, (n, t)
    return t


def _tiles(m, d, dff):
    return {
        "proj_m_f32": _tile(m, 1024),
        "proj_m_bf16": _tile(m, 2048),
        "mix_m": _tile(m, 256),
        "ffn_m": _tile(m, 1024), "ffn_f": _tile(dff, 1024),
        "ple_m": _tile(m, 512),
    }


def kernel(x, p, w_in, conv_w, hg_lb_logits, hg_norm_g, att_rel_bias, w_branch, w_o,
           w_ff1, w_ff2, w_ple_in, w_ple_gate, g_mix, g_ff, g_ple, g_final):
    depth = w_in.shape[0]
    b, s, d = x.shape
    m = b * s
    w = MIX_WIDTH
    assert s % ATT_GQ == 0 and s % HG_BLOCK == 0 and d % w == 0
    assert w_in.shape[2] == (REF_FIRST_GATE_TILE + 3 * d // w) * w

    lb_sm = jax.nn.softmax(hg_lb_logits.astype(F32), axis=0)
    lb_all = jnp.cumsum(lb_sm, axis=0) - lb_sm[0]

    col_scale = jnp.ones((w_in.shape[2],), F32).at[REF_AQ_TILE * w:(REF_AQ_TILE + 1) * w].set(
        HEAD_DIM ** -0.5 * LOG2_E)
    w_in_b = (w_in * col_scale).astype(BF16)
    w_branch_b, w_o_b = w_branch.astype(BF16), w_o.astype(BF16)
    w_ff1_b, w_ff2_b = w_ff1.astype(BF16), w_ff2.astype(BF16)
    w_gate_b, w_ple_b = w_ple_gate.astype(BF16), w_ple_in.astype(BF16)
    p2 = p.reshape(depth, m, p.shape[3])

    t = _tiles(m, d, w_ff1.shape[2])
    h = x.reshape(m, d)
    xn = h
    for l in range(depth):
        last = l == depth - 1
        proj, z = in_proj(xn, g_mix[l].reshape(1, d), w_in_b, l,
                          tm=t["proj_m_bf16"] if xn.dtype == BF16 else t["proj_m_f32"])
        proj3 = proj.reshape(b, s, proj.shape[1])
        o_hg = hgrn_branch(proj3, z.reshape(b, s, w), lb_all[l], HG_HEADS_PER_STEP)
        y_att = attention_branch(proj3, att_rel_bias[l])
        h1, hn = mix_out(proj, o_hg.reshape(m, w), y_att.reshape(m, w), h, conv_w[l], hg_norm_g[l],
                         w_branch_b, w_o_b, l, g_ff[l].reshape(1, d), seq_len=s, tm=t["mix_m"])
        y = ffn(hn, w_ff1_b, w_ff2_b, l, tm=t["ffn_m"], tf=t["ffn_f"])
        g_next = (g_final if last else g_mix[l + 1]).reshape(1, d)
        outs = ple_gate(h1, y, g_ple[l].reshape(1, d), w_gate_b, p2, w_ple_b, l, g_next, last, tm=t["ple_m"])
        if last:
            h, = outs
        else:
            h, xn = outs
    return h.reshape(b, s, d)
```

```python
import functools

import numpy as np
import jax
import jax.numpy as jnp
from jax import lax
from jax.experimental import pallas as pl
from jax.experimental.pallas import tpu as pltpu

F32 = jnp.float32
BF16 = jnp.bfloat16

EPS = 1e-6
LB_FLOOR = 1e-30
MASK_VALUE = -1e30
LOG2_E = 1.4426950408889634

MIX_WIDTH = 1024
CONV_K = 3
HEADS = 8
HEAD_DIM = 128
CHUNK = 64
LEFT_CHUNKS = 8
MAX_REL_DIST = 256

HG_BLOCK = 128
HG_LEVELS = 7
ATT_GROUP_CHUNKS = 4
ATT_GQ = ATT_GROUP_CHUNKS * CHUNK
ATT_GK = (ATT_GROUP_CHUNKS + LEFT_CHUNKS) * CHUNK
HG_HEADS_PER_STEP = 8
HG_UNROLL = 4

VMEM_LIMIT_BYTES = 58 * 1024 * 1024

COL_CB, COL_CC, COL_CH, COL_HQ, COL_HI, COL_HG, COL_AQ, COL_AK, COL_AV = range(9)
N_MIX_BLOCKS = 9
REF_HF_TILE = 4
REF_AQ_TILE = 7
REF_FIRST_GATE_TILE = 10


def _mix_block(proj, block):
    return proj.shape[-1] // MIX_WIDTH - N_MIX_BLOCKS + block


def _params(*sem):
    return pltpu.CompilerParams(dimension_semantics=sem, vmem_limit_bytes=VMEM_LIMIT_BYTES)


def _rms(x, g):
    ms = jnp.mean(x * x, axis=-1, keepdims=True)
    return x * lax.rsqrt(ms + EPS) * g


def _dot(a, b):
    return jnp.dot(a, b, preferred_element_type=F32)


def _dot_nt(a, b):
    return lax.dot_general(a, b, (((1,), (1,)), ((), ())), preferred_element_type=F32)


def _dot_tn(a, b):
    return lax.dot_general(a, b, (((0,), (0,)), ((), ())), preferred_element_type=F32)


def _sigmoid(x):
    return 1.0 / (1.0 + jnp.exp(-x))


def _in_proj_kernel(x_ref, g_ref, w_ref, ob_ref, of_ref, *scratch, normalize):
    j = pl.program_id(1)
    if normalize:
        xn_ref, = scratch

        @pl.when(j == 0)
        def _():
            xn_ref[...] = _rms(x_ref[...], g_ref[...]).astype(BF16)
    else:
        xn_ref = x_ref

    @pl.when(j != REF_HF_TILE)
    def _():
        ob_ref[...] = _dot(xn_ref[...], w_ref[0]).astype(BF16)

    @pl.when(j == REF_HF_TILE)
    def _():
        of_ref[...] = _dot(xn_ref[...], w_ref[0])


def in_proj(x, g, w, layer, tm):
    m, d = x.shape
    n = w.shape[2]
    tn = MIX_WIDTH
    n_gate_tiles = n // tn - REF_FIRST_GATE_TILE
    normalize = x.dtype != BF16

    def bf16_block(j):
        mixer = n_gate_tiles + jnp.where(j < REF_HF_TILE, j, jnp.maximum(j - 1, REF_HF_TILE - 1))
        return jnp.where(j >= REF_FIRST_GATE_TILE, j - REF_FIRST_GATE_TILE, mixer)

    return pl.pallas_call(
        functools.partial(_in_proj_kernel, normalize=normalize),
        grid=(m // tm, n // tn),
        in_specs=[
            pl.BlockSpec((tm, d), lambda i, j: (i, 0), pipeline_mode=None if normalize else pl.Buffered(1)),
            pl.BlockSpec((1, d), lambda i, j: (0, 0)),
            pl.BlockSpec((1, d, tn), lambda i, j: (layer, 0, j)),
        ],
        out_specs=[
            pl.BlockSpec((tm, tn), lambda i, j: (i, bf16_block(j))),
            pl.BlockSpec((tm, tn), lambda i, j: (i, 0)),
        ],
        out_shape=[
            jax.ShapeDtypeStruct((m, n - tn), BF16),
            jax.ShapeDtypeStruct((m, tn), F32),
        ],
        scratch_shapes=[pltpu.VMEM((tm, d), BF16)] if normalize else [],
        compiler_params=_params("parallel", "arbitrary"),
        name="in_proj",
    )(x, g, w)


def _hgrn_level_tables():
    t = np.arange(HG_BLOCK)
    x = t[:, None] ^ t[None, :]
    lvl = np.where(t[:, None] > t[None, :], np.floor(np.log2(np.maximum(x, 1))).astype(np.int64) + 1, -1)
    lvl = np.where(t[:, None] == t[None, :], 0, lvl).astype(np.int32)
    tri = (t[:, None] >= t[None, :]).astype(np.float32)
    return lvl, tri


def _split_point_rows(g, level):
    size, half = 1 << level, 1 << (level - 1)
    n = g.shape[0]
    if size >= 8:
        parts = [jnp.broadcast_to(g[b * size + half - 1:b * size + half, :], (size, g.shape[1]))
                 for b in range(n // size)]
        return jnp.concatenate(parts, axis=0)
    sub = lax.broadcasted_iota(jnp.int32, (8, g.shape[1]), 0)
    parts = []
    for v in range(n // 8):
        acc = None
        for b in range(8 // size):
            r = 8 * v + b * size + half - 1
            cand = jnp.broadcast_to(g[r:r + 1, :], (8, g.shape[1]))
            acc = cand if acc is None else jnp.where(sub >= b * size, cand, acc)
        parts.append(acc)
    return jnp.concatenate(parts, axis=0)


def _query_or_key_rows(q, key, level):
    size, half = 1 << level, 1 << (level - 1)
    n = q.shape[0]
    if half >= 8:
        parts = []
        for b in range(n // size):
            parts += [key[b * size:b * size + half], q[b * size + half:(b + 1) * size]]
        return jnp.concatenate(parts, axis=0)
    row = lax.broadcasted_iota(jnp.int32, q.shape, 0)
    return jnp.where((row & half) != 0, q, key)


def _neg_abs(x):
    bits = lax.bitcast_convert_type(x, jnp.uint32) | jnp.uint32(0x80000000)
    return lax.bitcast_convert_type(bits, F32)


def _hgrn_block(qs, vs, zs, lb_floors, one_m_lbs, tri, lvl, sts):
    L = HG_BLOCK
    heads = range(len(qs))
    fs, keys, g2s = [], [], []
    for h in heads:
        e = jnp.exp(_neg_abs(zs[h]))
        inv = 1.0 / (1.0 + e)
        einv = e * inv
        pos = zs[h] >= 0
        f = lb_floors[h] + one_m_lbs[h] * jnp.where(pos, inv, einv)
        fs.append(f)
        keys.append(one_m_lbs[h] * jnp.where(pos, einv, inv))
        lf2 = jnp.log2(f)
        hi = lf2.astype(BF16)
        lo = (lf2 - hi.astype(F32)).astype(BF16)
        c2 = _dot(tri, jnp.concatenate([hi, lo], axis=1))
        g2s.append(c2[:, :HEAD_DIM] + c2[:, HEAD_DIM:])

    scores = []
    for h in heads:
        p01 = _dot_nt(jnp.concatenate([qs[h], qs[h] * fs[h]], axis=0).astype(BF16), keys[h].astype(BF16))
        scores.append(jnp.where(lvl == 1, p01[L:], jnp.where(lvl == 0, p01[:L], 0.0)))
    for level in range(2, HG_LEVELS + 1):
        for h in heads:
            d = jnp.exp2(_neg_abs(g2s[h] - _split_point_rows(g2s[h], level)))
            x = (_query_or_key_rows(qs[h], keys[h], level) * d).astype(BF16)
            scores[h] = jnp.where(lvl == level, _dot_nt(x, x), scores[h])

    outs, new_sts = [], []
    for h in heads:
        g_tot = g2s[h][L - 1:L, :]
        outs.append(_dot(scores[h].astype(BF16), vs[h])
                    + _dot_nt((qs[h] * jnp.exp2(g2s[h])).astype(BF16), sts[h].astype(BF16)))
        kd = (keys[h] * jnp.exp2(g_tot - g2s[h])).astype(BF16)
        new_sts.append(sts[h] * jnp.exp2(g_tot) + _dot_tn(vs[h], kd))
    return outs, new_sts


def _hgrn_kernel(q_ref, v_ref, z_ref, lb_ref, tri_ref, lvl_ref, o_ref, st_ref):
    L = HG_BLOCK
    n_blocks = q_ref.shape[1] // L
    n_heads = q_ref.shape[2] // HEAD_DIM
    head_cols = [slice(hh * HEAD_DIM, (hh + 1) * HEAD_DIM) for hh in range(n_heads)]
    lb = lb_ref[0]
    lb_floors = [jnp.maximum(lb[:, c], LB_FLOOR) for c in head_cols]
    one_m_lbs = [1.0 - lb[:, c] for c in head_cols]
    tri = tri_ref[...]
    lvl = lvl_ref[...]
    st_ref[...] = jnp.zeros(st_ref.shape, F32)

    def body(n, carry):
        rows = pl.ds(pl.multiple_of(n * L, L), L)
        outs, new_sts = _hgrn_block(
            [q_ref[0, rows, c].astype(F32) for c in head_cols], [v_ref[0, rows, c] for c in head_cols],
            [z_ref[0, rows, c] for c in head_cols], lb_floors, one_m_lbs, tri, lvl,
            [st_ref[hh] for hh in range(n_heads)])
        for hh, c in enumerate(head_cols):
            st_ref[hh] = new_sts[hh]
            o_ref[0, rows, c] = outs[hh].astype(BF16)
        return carry

    lax.fori_loop(0, n_blocks, body, 0, unroll=HG_UNROLL)


def hgrn_branch(proj, z, lb, heads_per_step):
    b, s, _ = proj.shape
    lvl, tri = _hgrn_level_tables()
    width = heads_per_step * HEAD_DIM
    per = MIX_WIDTH // width

    def col(block):
        return pl.BlockSpec((1, s, width), lambda bi, h: (bi, 0, _mix_block(proj, block) * per + h))

    const = lambda bi, h: (0, 0)
    return pl.pallas_call(
        _hgrn_kernel,
        grid=(b, per),
        in_specs=[col(COL_HQ), col(COL_HI),
                  pl.BlockSpec((1, s, width), lambda bi, h: (bi, 0, h)),
                  pl.BlockSpec((1, 1, width), lambda bi, h: (h, 0, 0)),
                  pl.BlockSpec((HG_BLOCK, HG_BLOCK), const),
                  pl.BlockSpec((HG_BLOCK, HG_BLOCK), const)],
        out_specs=pl.BlockSpec((1, s, width), lambda bi, h: (bi, 0, h)),
        out_shape=jax.ShapeDtypeStruct((b, s, MIX_WIDTH), BF16),
        scratch_shapes=[pltpu.VMEM((heads_per_step, HEAD_DIM, HEAD_DIM), F32)],
        compiler_params=_params("parallel", "parallel"),
        name="hgrn_branch",
    )(proj, proj, z, lb.reshape(per, 1, width), jnp.asarray(tri, BF16), jnp.asarray(lvl))


def _att_group_bias(rel_table):
    heads = rel_table.shape[0]
    table = rel_table.astype(F32) * LOG2_E
    band = np.arange(ATT_GK)[None, :] // CHUNK - np.arange(ATT_GQ)[:, None] // CHUNK
    valid = (band >= 0) & (band <= LEFT_CHUNKS)
    lo, hi = LEFT_CHUNKS * CHUNK - (ATT_GK - 1), LEFT_CHUNKS * CHUNK + ATT_GQ - 1
    assert -MAX_REL_DIST <= lo and hi >= MAX_REL_DIST
    period = hi - lo + 2
    wr = jnp.concatenate([jnp.broadcast_to(table[:, 2 * MAX_REL_DIST:], (heads, hi - MAX_REL_DIST)),
                          jnp.flip(table[:, lo + MAX_REL_DIST:], axis=1),
                          jnp.zeros((heads, 1), F32)], axis=1)
    m = jnp.tile(wr, (1, ATT_GQ))[:, :ATT_GQ * (period - 1)].reshape(heads, ATT_GQ, period - 1)
    bias = m[:, :, ATT_GQ - 1:ATT_GQ - 1 + ATT_GK]
    return jnp.where(jnp.asarray(valid)[None], bias, MASK_VALUE)


def _att_kernel(q_ref, k_ref, v_ref, bias_ref, o_ref):
    n_groups = q_ref.shape[1] // ATT_GQ

    spans = []
    for g in range(n_groups):
        q0 = g * ATT_GQ
        k0 = max(0, q0 + ATT_GQ - ATT_GK)
        spans.append((q0, k0, q0 + ATT_GQ - k0))
    scores, probs = {}, {}
    for t in range(n_groups + 2):
        if t < n_groups:
            q0, k0, nk = spans[t]
            scores[t] = (_dot_nt(q_ref[0, q0:q0 + ATT_GQ, :], k_ref[0, k0:k0 + nk, :])
                         + bias_ref[0, :, ATT_GK - nk:])
        if 0 <= t - 1 < n_groups:
            s = scores.pop(t - 1)
            probs[t - 1] = jnp.exp2(s - jnp.max(s, axis=-1, keepdims=True)).astype(BF16)
        if 0 <= t - 2 < n_groups:
            q0, k0, nk = spans[t - 2]
            v = v_ref[0, k0:k0 + nk, :]
            ov = _dot(probs.pop(t - 2), jnp.concatenate([v, jnp.ones_like(v)], axis=1))
            o = ov[:, :HEAD_DIM] / ov[:, HEAD_DIM:]
            o_ref[0, q0:q0 + ATT_GQ, :] = o.astype(BF16)


def attention_branch(proj, rel_table):
    b, s, _ = proj.shape
    bias = _att_group_bias(rel_table)

    def col(block):
        return pl.BlockSpec((1, s, HEAD_DIM), lambda h, bi: (bi, 0, _mix_block(proj, block) * HEADS + h))

    return pl.pallas_call(
        _att_kernel,
        grid=(HEADS, b),
        in_specs=[col(COL_AQ), col(COL_AK), col(COL_AV),
                  pl.BlockSpec((1, ATT_GQ, ATT_GK), lambda h, bi: (h, 0, 0))],
        out_specs=pl.BlockSpec((1, s, HEAD_DIM), lambda h, bi: (bi, 0, h)),
        out_shape=jax.ShapeDtypeStruct((b, s, MIX_WIDTH), BF16),
        compiler_params=_params("parallel", "parallel"),
        name="attention_branch",
    )(proj, proj, proj, bias)


CONV_HALO = 8


def _conv_branch_tile(cb, cc, ch, cc_prev, ch_prev, w, first_in_seq):
    u = cc.astype(F32) * ch.astype(F32)
    prev = jnp.where(first_in_seq, 0.0, cc_prev.astype(F32) * ch_prev.astype(F32))
    before1 = jnp.broadcast_to(prev[CONV_HALO - 1:CONV_HALO, :], u.shape)
    before2 = jnp.broadcast_to(prev[CONV_HALO - 2:CONV_HALO - 1, :], u.shape)
    row = lax.broadcasted_iota(jnp.int32, u.shape, 0)
    u1 = jnp.where(row >= 1, pltpu.roll(u, 1, 0), before1)
    u2 = jnp.where(row >= 2, pltpu.roll(u, 2, 0), jnp.where(row == 1, before1, before2))
    return cb.astype(F32) * (u2 * w[0:1, :] + u1 * w[1:2, :] + u * w[2:3, :])


def _hgrn_norm_gate(o, gate, ng):
    parts = []
    for hh in range(o.shape[1] // HEAD_DIM):
        cols = slice(hh * HEAD_DIM, (hh + 1) * HEAD_DIM)
        g = gate[:, cols]
        parts.append(_rms(o[:, cols], ng) * (g * _sigmoid(g)))
    return jnp.concatenate(parts, axis=1)


def _mix_out_kernel(cb_ref, cc_ref, ch_ref, ccp_ref, chp_ref, cw_ref, oh_ref, hg_ref, ng_ref, ya_ref,
                    gc_ref, gh_ref, ga_ref, h_ref, wb_ref, wo_ref, gn_ref, h1_ref, hn_ref, *, tiles_per_seq):
    first_in_seq = pl.program_id(0) % tiles_per_seq == 0
    y_conv = _conv_branch_tile(cb_ref[...], cc_ref[...], ch_ref[...], ccp_ref[...], chp_ref[...],
                               cw_ref[...], first_in_seq)
    y_hg = _hgrn_norm_gate(oh_ref[...].astype(F32), hg_ref[...].astype(F32), ng_ref[...])
    merged = _sigmoid(gc_ref[...].astype(F32)) * _dot(y_conv.astype(BF16), wb_ref[0, 0])
    merged += _sigmoid(gh_ref[...].astype(F32)) * _dot(y_hg.astype(BF16), wb_ref[0, 1])
    merged += _sigmoid(ga_ref[...].astype(F32)) * _dot(ya_ref[...], wb_ref[0, 2])
    h1 = h_ref[...] + _dot(merged.astype(BF16), wo_ref[0])
    h1_ref[...] = h1
    hn_ref[...] = _rms(h1, gn_ref[...]).astype(BF16)


def mix_out(proj, o_hg, y_att, h, conv_w, norm_g, w_branch, w_o, layer, g_next, seq_len, tm):
    m, d = h.shape
    w = MIX_WIDTH
    assert seq_len % tm == 0 and tm % CONV_HALO == 0
    rows = lambda width: pl.BlockSpec((tm, width), lambda i: (i, 0))
    mix = lambda block: pl.BlockSpec((tm, w), lambda i: (i, _mix_block(proj, block)))
    halo = lambda block: pl.BlockSpec(
        (CONV_HALO, w), lambda i: (jnp.maximum(i * (tm // CONV_HALO) - 1, 0), _mix_block(proj, block)))
    gate = lambda branch: pl.BlockSpec((tm, d), lambda i: (i, branch))
    const = lambda i: (0, 0)
    resident = pl.Buffered(1)
    return pl.pallas_call(
        functools.partial(_mix_out_kernel, tiles_per_seq=seq_len // tm),
        grid=(m // tm,),
        in_specs=[mix(COL_CB), mix(COL_CC), mix(COL_CH), halo(COL_CC), halo(COL_CH),
                  pl.BlockSpec((CONV_K, w), const),
                  rows(w), mix(COL_HG), pl.BlockSpec((1, HEAD_DIM), const), rows(w),
                  gate(0), gate(1), gate(2), rows(d),
                  pl.BlockSpec((1,) + w_branch.shape[1:], lambda i: (layer, 0, 0, 0), pipeline_mode=resident),
                  pl.BlockSpec((1,) + w_o.shape[1:], lambda i: (layer, 0, 0), pipeline_mode=resident),
                  pl.BlockSpec((1, d), const)],
        out_specs=[rows(d), rows(d)],
        out_shape=[jax.ShapeDtypeStruct((m, d), F32), jax.ShapeDtypeStruct((m, d), BF16)],
        compiler_params=_params("parallel"),
        name="mix_out",
    )(proj, proj, proj, proj, proj, conv_w, o_hg, proj, norm_g.reshape(1, HEAD_DIM), y_att,
      proj, proj, proj, h, w_branch, w_o, g_next)


def _ffn_kernel(hn_ref, w1_ref, w2_ref, o_ref, acc_ref, *, n_steps):
    f = pl.program_id(1)
    last = n_steps - 1

    def part():
        hid = jnp.square(jnp.maximum(_dot(hn_ref[...], w1_ref[0]), 0.0))
        return _dot(hid.astype(BF16), w2_ref[0])

    if n_steps == 1:
        o_ref[...] = part().astype(BF16)
        return

    @pl.when(f == 0)
    def _():
        acc_ref[...] = part()

    @pl.when((f > 0) & (f < last))
    def _():
        acc_ref[...] += part()

    @pl.when(f == last)
    def _():
        o_ref[...] = (acc_ref[...] + part()).astype(BF16)


def ffn(hn, w1, w2, layer, tm, tf):
    m, d = hn.shape
    dff = w1.shape[2]
    return pl.pallas_call(
        functools.partial(_ffn_kernel, n_steps=dff // tf),
        grid=(m // tm, dff // tf),
        in_specs=[pl.BlockSpec((tm, d), lambda i, f: (i, 0)),
                  pl.BlockSpec((1, d, tf), lambda i, f: (layer, 0, f)),
                  pl.BlockSpec((1, tf, d), lambda i, f: (layer, f, 0))],
        out_specs=pl.BlockSpec((tm, d), lambda i, f: (i, 0)),
        out_shape=jax.ShapeDtypeStruct((m, d), BF16),
        scratch_shapes=[pltpu.VMEM((tm, d), F32)],
        compiler_params=_params("parallel", "arbitrary"),
        name="ffn",
    )(hn, w1, w2)


def _ple_kernel(h_ref, y_ref, g_ref, wg_ref, p_ref, wp_ref, gn_ref, *out_refs, last):
    x = h_ref[...] + y_ref[...].astype(F32)
    gate = _sigmoid(_dot(_rms(x, g_ref[...]).astype(BF16), wg_ref[0]))
    out = x + gate * _dot(p_ref[0].astype(BF16), wp_ref[0])
    if last:
        out_refs[0][...] = _rms(out, gn_ref[...])
    else:
        out_refs[0][...] = out
        out_refs[1][...] = _rms(out, gn_ref[...]).astype(BF16)


def ple_gate(h, y, g, w_gate, p, w_p, layer, g_next, last, tm):
    m, d = h.shape
    const = lambda i: (0, 0)
    slab = lambda i: (layer, 0, 0)
    rows = pl.BlockSpec((tm, d), lambda i: (i, 0))
    resident = pl.Buffered(1)
    if last:
        out_specs, out_shape = [rows], [jax.ShapeDtypeStruct((m, d), F32)]
    else:
        out_specs = [rows, rows]
        out_shape = [jax.ShapeDtypeStruct((m, d), F32), jax.ShapeDtypeStruct((m, d), BF16)]
    return pl.pallas_call(
        functools.partial(_ple_kernel, last=last),
        grid=(m // tm,),
        in_specs=[rows, rows,
                  pl.BlockSpec((1, d), const),
                  pl.BlockSpec((1,) + w_gate.shape[1:], slab, pipeline_mode=resident),
                  pl.BlockSpec((1, tm, p.shape[2]), lambda i: (layer, i, 0)),
                  pl.BlockSpec((1,) + w_p.shape[1:], slab, pipeline_mode=resident),
                  pl.BlockSpec((1, d), const)],
        out_specs=out_specs,
        out_shape=out_shape,
        compiler_params=_params("parallel"),
        name="ple_gate",
    )(h, y, g, w_gate, p, w_p, g_next)


def _tile(n, target):
    t = min(n, target)
    assert n % t == 0, (n, t)
    return t


def _tiles(m, d, dff):
    return {
        "proj_m_f32": _tile(m, 1024),
        "proj_m_bf16": _tile(m, 2048),
        "mix_m": _tile(m, 256),
        "ffn_m": _tile(m, 1024), "ffn_f": _tile(dff, 1024),
        "ple_m": _tile(m, 512),
    }


def kernel(x, p, w_in, conv_w, hg_lb_logits, hg_norm_g, att_rel_bias, w_branch, w_o,
           w_ff1, w_ff2, w_ple_in, w_ple_gate, g_mix, g_ff, g_ple, g_final):
    depth = w_in.shape[0]
    b, s, d = x.shape
    m = b * s
    w = MIX_WIDTH
    assert s % ATT_GQ == 0 and s % HG_BLOCK == 0 and d % w == 0
    assert w_in.shape[2] == (REF_FIRST_GATE_TILE + 3 * d // w) * w

    lb_sm = jax.nn.softmax(hg_lb_logits.astype(F32), axis=0)
    lb_all = jnp.cumsum(lb_sm, axis=0) - lb_sm[0]

    col_scale = jnp.ones((w_in.shape[2],), F32).at[REF_AQ_TILE * w:(REF_AQ_TILE + 1) * w].set(
        HEAD_DIM ** -0.5 * LOG2_E)
    w_in_b = (w_in * col_scale).astype(BF16)
    w_branch_b, w_o_b = w_branch.astype(BF16), w_o.astype(BF16)
    w_ff1_b, w_ff2_b = w_ff1.astype(BF16), w_ff2.astype(BF16)
    w_gate_b, w_ple_b = w_ple_gate.astype(BF16), w_ple_in.astype(BF16)
    p2 = p.reshape(depth, m, p.shape[3])

    t = _tiles(m, d, w_ff1.shape[2])
    h = x.reshape(m, d)
    xn = h
    for l in range(depth):
        last = l == depth - 1
        proj, z = in_proj(xn, g_mix[l].reshape(1, d), w_in_b, l,
                          tm=t["proj_m_bf16"] if xn.dtype == BF16 else t["proj_m_f32"])
        proj3 = proj.reshape(b, s, proj.shape[1])
        o_hg = hgrn_branch(proj3, z.reshape(b, s, w), lb_all[l], HG_HEADS_PER_STEP)
        y_att = attention_branch(proj3, att_rel_bias[l])
        h1, hn = mix_out(proj, o_hg.reshape(m, w), y_att.reshape(m, w), h, conv_w[l], hg_norm_g[l],
                         w_branch_b, w_o_b, l, g_ff[l].reshape(1, d), seq_len=s, tm=t["mix_m"])
        y = ffn(hn, w_ff1_b, w_ff2_b, l, tm=t["ffn_m"], tf=t["ffn_f"])
        g_next = (g_final if last else g_mix[l + 1]).reshape(1, d)
        outs = ple_gate(h1, y, g_ple[l].reshape(1, d), w_gate_b, p2, w_ple_b, l, g_next, last, tm=t["ple_m"])
        if last:
            h, = outs
        else:
            h, xn = outs
    return h.reshape(b, s, d)
```

```python
import functools

import numpy as np
import jax
import jax.numpy as jnp
from jax import lax
from jax.experimental import pallas as pl
from jax.experimental.pallas import tpu as pltpu

F32 = jnp.float32
BF16 = jnp.bfloat16

EPS = 1e-6
LB_FLOOR = 1e-30
MASK_VALUE = -1e30
LOG2_E = 1.4426950408889634

MIX_WIDTH = 1024
CONV_K = 3
HEADS = 8
HEAD_DIM = 128
CHUNK = 64
LEFT_CHUNKS = 8
MAX_REL_DIST = 256

HG_BLOCK = 128
HG_LEVELS = 7
ATT_GROUP_CHUNKS = 4
ATT_GQ = ATT_GROUP_CHUNKS * CHUNK
ATT_GK = (ATT_GROUP_CHUNKS + LEFT_CHUNKS) * CHUNK
HG_HEADS_PER_STEP = 8
HG_UNROLL = 4

VMEM_LIMIT_BYTES = 58 * 1024 * 1024

COL_CB, COL_CC, COL_CH, COL_HQ, COL_HI, COL_HG, COL_AQ, COL_AK, COL_AV = range(9)
N_MIX_BLOCKS = 9
REF_HF_TILE = 4
REF_AQ_TILE = 7
REF_FIRST_GATE_TILE = 10


def _mix_block(proj, block):
    return proj.shape[-1] // MIX_WIDTH - N_MIX_BLOCKS + block


def _params(*sem):
    return pltpu.CompilerParams(dimension_semantics=sem, vmem_limit_bytes=VMEM_LIMIT_BYTES)


def _rms(x, g):
    ms = jnp.mean(x * x, axis=-1, keepdims=True)
    return x * lax.rsqrt(ms + EPS) * g


def _dot(a, b):
    return jnp.dot(a, b, preferred_element_type=F32)


def _dot_nt(a, b):
    return lax.dot_general(a, b, (((1,), (1,)), ((), ())), preferred_element_type=F32)


def _dot_tn(a, b):
    return lax.dot_general(a, b, (((0,), (0,)), ((), ())), preferred_element_type=F32)


def _sigmoid(x):
    return 1.0 / (1.0 + jnp.exp(-x))


def _in_proj_kernel(x_ref, g_ref, w_ref, ob_ref, of_ref, *scratch, normalize):
    j = pl.program_id(1)
    if normalize:
        xn_ref, = scratch

        @pl.when(j == 0)
        def _():
            xn_ref[...] = _rms(x_ref[...], g_ref[...]).astype(BF16)
    else:
        xn_ref = x_ref

    @pl.when(j != REF_HF_TILE)
    def _():
        ob_ref[...] = _dot(xn_ref[...], w_ref[0, 0]).astype(BF16)

    @pl.when(j == REF_HF_TILE)
    def _():
        of_ref[...] = _dot(xn_ref[...], w_ref[0, 0])


def in_proj(x, g, w, layer, tm):
    m, d = x.shape
    tn = MIX_WIDTH
    n = w.shape[1] * tn
    n_gate_tiles = n // tn - REF_FIRST_GATE_TILE
    normalize = x.dtype != BF16

    def bf16_block(j):
        mixer = n_gate_tiles + jnp.where(j < REF_HF_TILE, j, jnp.maximum(j - 1, REF_HF_TILE - 1))
        return jnp.where(j >= REF_FIRST_GATE_TILE, j - REF_FIRST_GATE_TILE, mixer)

    return pl.pallas_call(
        functools.partial(_in_proj_kernel, normalize=normalize),
        grid=(m // tm, n // tn),
        in_specs=[
            pl.BlockSpec((tm, d), lambda i, j: (i, 0), pipeline_mode=None if normalize else pl.Buffered(1)),
            pl.BlockSpec((1, d), lambda i, j: (0, 0)),
            pl.BlockSpec((1, 1, d, tn), lambda i, j: (layer, j, 0, 0)),
        ],
        out_specs=[
            pl.BlockSpec((tm, tn), lambda i, j: (i, bf16_block(j))),
            pl.BlockSpec((tm, tn), lambda i, j: (i, 0)),
        ],
        out_shape=[
            jax.ShapeDtypeStruct((m, n - tn), BF16),
            jax.ShapeDtypeStruct((m, tn), F32),
        ],
        scratch_shapes=[pltpu.VMEM((tm, d), BF16)] if normalize else [],
        compiler_params=_params("parallel", "arbitrary"),
        name="in_proj",
    )(x, g, w)


def _hgrn_level_tables():
    t = np.arange(HG_BLOCK)
    x = t[:, None] ^ t[None, :]
    lvl = np.where(t[:, None] > t[None, :], np.floor(np.log2(np.maximum(x, 1))).astype(np.int64) + 1, -1)
    lvl = np.where(t[:, None] == t[None, :], 0, lvl).astype(np.int32)
    tri = (t[:, None] >= t[None, :]).astype(np.float32)
    return lvl, tri


def _split_point_rows(g, level):
    size, half = 1 << level, 1 << (level - 1)
    n = g.shape[0]
    if size >= 8:
        parts = [jnp.broadcast_to(g[b * size + half - 1:b * size + half, :], (size, g.shape[1]))
                 for b in range(n // size)]
        return jnp.concatenate(parts, axis=0)
    sub = lax.broadcasted_iota(jnp.int32, (8, g.shape[1]), 0)
    parts = []
    for v in range(n // 8):
        acc = None
        for b in range(8 // size):
            r = 8 * v + b * size + half - 1
            cand = jnp.broadcast_to(g[r:r + 1, :], (8, g.shape[1]))
            acc = cand if acc is None else jnp.where(sub >= b * size, cand, acc)
        parts.append(acc)
    return jnp.concatenate(parts, axis=0)


def _query_or_key_rows(q, key, level):
    size, half = 1 << level, 1 << (level - 1)
    n = q.shape[0]
    if half >= 8:
        parts = []
        for b in range(n // size):
            parts += [key[b * size:b * size + half], q[b * size + half:(b + 1) * size]]
        return jnp.concatenate(parts, axis=0)
    row = lax.broadcasted_iota(jnp.int32, q.shape, 0)
    return jnp.where((row & half) != 0, q, key)


def _neg_abs(x):
    bits = lax.bitcast_convert_type(x, jnp.uint32) | jnp.uint32(0x80000000)
    return lax.bitcast_convert_type(bits, F32)


def _hgrn_block(qs, vs, zs, lb_floors, one_m_lbs, tri, lvl, sts):
    L = HG_BLOCK
    heads = range(len(qs))
    fs, keys, g2s = [], [], []
    for h in heads:
        e = jnp.exp(_neg_abs(zs[h]))
        inv = 1.0 / (1.0 + e)
        einv = e * inv
        pos = zs[h] >= 0
        f = lb_floors[h] + one_m_lbs[h] * jnp.where(pos, inv, einv)
        fs.append(f)
        keys.append(one_m_lbs[h] * jnp.where(pos, einv, inv))
        lf2 = jnp.log2(f)
        hi = lf2.astype(BF16)
        lo = (lf2 - hi.astype(F32)).astype(BF16)
        c2 = _dot(tri, jnp.concatenate([hi, lo], axis=1))
        g2s.append(c2[:, :HEAD_DIM] + c2[:, HEAD_DIM:])

    scores = []
    for h in heads:
        p01 = _dot_nt(jnp.concatenate([qs[h], qs[h] * fs[h]], axis=0).astype(BF16), keys[h].astype(BF16))
        scores.append(jnp.where(lvl == 1, p01[L:], jnp.where(lvl == 0, p01[:L], 0.0)))
    for level in range(2, HG_LEVELS + 1):
        for h in heads:
            d = jnp.exp2(_neg_abs(g2s[h] - _split_point_rows(g2s[h], level)))
            x = (_query_or_key_rows(qs[h], keys[h], level) * d).astype(BF16)
            scores[h] = jnp.where(lvl == level, _dot_nt(x, x), scores[h])

    outs, new_sts = [], []
    for h in heads:
        g_tot = g2s[h][L - 1:L, :]
        outs.append(_dot(scores[h].astype(BF16), vs[h])
                    + _dot_nt((qs[h] * jnp.exp2(g2s[h])).astype(BF16), sts[h].astype(BF16)))
        kd = (keys[h] * jnp.exp2(g_tot - g2s[h])).astype(BF16)
        new_sts.append(sts[h] * jnp.exp2(g_tot) + _dot_tn(vs[h], kd))
    return outs, new_sts


def _hgrn_kernel(q_ref, v_ref, z_ref, lb_ref, tri_ref, lvl_ref, o_ref, st_ref):
    L = HG_BLOCK
    n_blocks = q_ref.shape[1] // L
    n_heads = q_ref.shape[2] // HEAD_DIM
    head_cols = [slice(hh * HEAD_DIM, (hh + 1) * HEAD_DIM) for hh in range(n_heads)]
    lb = lb_ref[0]
    lb_floors = [jnp.maximum(lb[:, c], LB_FLOOR) for c in head_cols]
    one_m_lbs = [1.0 - lb[:, c] for c in head_cols]
    tri = tri_ref[...]
    lvl = lvl_ref[...]
    st_ref[...] = jnp.zeros(st_ref.shape, F32)

    def body(n, carry):
        rows = pl.ds(pl.multiple_of(n * L, L), L)
        outs, new_sts = _hgrn_block(
            [q_ref[0, rows, c].astype(F32) for c in head_cols], [v_ref[0, rows, c] for c in head_cols],
            [z_ref[0, rows, c] for c in head_cols], lb_floors, one_m_lbs, tri, lvl,
            [st_ref[hh] for hh in range(n_heads)])
        for hh, c in enumerate(head_cols):
            st_ref[hh] = new_sts[hh]
            o_ref[0, rows, c] = outs[hh].astype(BF16)
        return carry

    lax.fori_loop(0, n_blocks, body, 0, unroll=HG_UNROLL)


def hgrn_branch(proj, z, lb, heads_per_step):
    b, s, _ = proj.shape
    lvl, tri = _hgrn_level_tables()
    width = heads_per_step * HEAD_DIM
    per = MIX_WIDTH // width

    def col(block):
        return pl.BlockSpec((1, s, width), lambda bi, h: (bi, 0, _mix_block(proj, block) * per + h))

    const = lambda bi, h: (0, 0)
    return pl.pallas_call(
        _hgrn_kernel,
        grid=(b, per),
        in_specs=[col(COL_HQ), col(COL_HI),
                  pl.BlockSpec((1, s, width), lambda bi, h: (bi, 0, h)),
                  pl.BlockSpec((1, 1, width), lambda bi, h: (h, 0, 0)),
                  pl.BlockSpec((HG_BLOCK, HG_BLOCK), const),
                  pl.BlockSpec((HG_BLOCK, HG_BLOCK), const)],
        out_specs=pl.BlockSpec((1, s, width), lambda bi, h: (bi, 0, h)),
        out_shape=jax.ShapeDtypeStruct((b, s, MIX_WIDTH), BF16),
        scratch_shapes=[pltpu.VMEM((heads_per_step, HEAD_DIM, HEAD_DIM), F32)],
        compiler_params=_params("parallel", "parallel"),
        name="hgrn_branch",
    )(proj, proj, z, lb.reshape(per, 1, width), jnp.asarray(tri, BF16), jnp.asarray(lvl))


def _att_group_bias(rel_table):
    heads = rel_table.shape[0]
    table = rel_table.astype(F32) * LOG2_E
    band = np.arange(ATT_GK)[None, :] // CHUNK - np.arange(ATT_GQ)[:, None] // CHUNK
    valid = (band >= 0) & (band <= LEFT_CHUNKS)
    lo, hi = LEFT_CHUNKS * CHUNK - (ATT_GK - 1), LEFT_CHUNKS * CHUNK + ATT_GQ - 1
    assert -MAX_REL_DIST <= lo and hi >= MAX_REL_DIST
    period = hi - lo + 2
    wr = jnp.concatenate([jnp.broadcast_to(table[:, 2 * MAX_REL_DIST:], (heads, hi - MAX_REL_DIST)),
                          jnp.flip(table[:, lo + MAX_REL_DIST:], axis=1),
                          jnp.zeros((heads, 1), F32)], axis=1)
    m = jnp.tile(wr, (1, ATT_GQ))[:, :ATT_GQ * (period - 1)].reshape(heads, ATT_GQ, period - 1)
    bias = m[:, :, ATT_GQ - 1:ATT_GQ - 1 + ATT_GK]
    return jnp.where(jnp.asarray(valid)[None], bias, MASK_VALUE)


def _att_kernel(q_ref, k_ref, v_ref, bias_ref, o_ref):
    n_groups = q_ref.shape[1] // ATT_GQ

    spans = []
    for g in range(n_groups):
        q0 = g * ATT_GQ
        k0 = max(0, q0 + ATT_GQ - ATT_GK)
        spans.append((q0, k0, q0 + ATT_GQ - k0))
    scores, probs = {}, {}
    for t in range(n_groups + 2):
        if t < n_groups:
            q0, k0, nk = spans[t]
            scores[t] = (_dot_nt(q_ref[0, q0:q0 + ATT_GQ, :], k_ref[0, k0:k0 + nk, :])
                         + bias_ref[0, :, ATT_GK - nk:])
        if 0 <= t - 1 < n_groups:
            s = scores.pop(t - 1)
            probs[t - 1] = jnp.exp2(s - jnp.max(s, axis=-1, keepdims=True)).astype(BF16)
        if 0 <= t - 2 < n_groups:
            q0, k0, nk = spans[t - 2]
            v = v_ref[0, k0:k0 + nk, :]
            ov = _dot(probs.pop(t - 2), jnp.concatenate([v, jnp.ones_like(v)], axis=1))
            o = ov[:, :HEAD_DIM] / ov[:, HEAD_DIM:]
            o_ref[0, q0:q0 + ATT_GQ, :] = o.astype(BF16)


def attention_branch(proj, rel_table):
    b, s, _ = proj.shape
    bias = _att_group_bias(rel_table)

    def col(block):
        return pl.BlockSpec((1, s, HEAD_DIM), lambda h, bi: (bi, 0, _mix_block(proj, block) * HEADS + h))

    return pl.pallas_call(
        _att_kernel,
        grid=(HEADS, b),
        in_specs=[col(COL_AQ), col(COL_AK), col(COL_AV),
                  pl.BlockSpec((1, ATT_GQ, ATT_GK), lambda h, bi: (h, 0, 0))],
        out_specs=pl.BlockSpec((1, s, HEAD_DIM), lambda h, bi: (bi, 0, h)),
        out_shape=jax.ShapeDtypeStruct((b, s, MIX_WIDTH), BF16),
        compiler_params=_params("parallel", "parallel"),
        name="attention_branch",
    )(proj, proj, proj, bias)


CONV_HALO = 8


def _conv_branch_tile(cb, cc, ch, cc_prev, ch_prev, w, first_in_seq):
    u = cc.astype(F32) * ch.astype(F32)
    prev = jnp.where(first_in_seq, 0.0, cc_prev.astype(F32) * ch_prev.astype(F32))
    before1 = jnp.broadcast_to(prev[CONV_HALO - 1:CONV_HALO, :], u.shape)
    before2 = jnp.broadcast_to(prev[CONV_HALO - 2:CONV_HALO - 1, :], u.shape)
    row = lax.broadcasted_iota(jnp.int32, u.shape, 0)
    u1 = jnp.where(row >= 1, pltpu.roll(u, 1, 0), before1)
    u2 = jnp.where(row >= 2, pltpu.roll(u, 2, 0), jnp.where(row == 1, before1, before2))
    return cb.astype(F32) * (u2 * w[0:1, :] + u1 * w[1:2, :] + u * w[2:3, :])


def _hgrn_norm_gate(o, gate, ng):
    parts = []
    for hh in range(o.shape[1] // HEAD_DIM):
        cols = slice(hh * HEAD_DIM, (hh + 1) * HEAD_DIM)
        g = gate[:, cols]
        parts.append(_rms(o[:, cols], ng) * (g * _sigmoid(g)))
    return jnp.concatenate(parts, axis=1)


def _mix_out_kernel(cb_ref, cc_ref, ch_ref, ccp_ref, chp_ref, cw_ref, oh_ref, hg_ref, ng_ref, ya_ref,
                    gc_ref, gh_ref, ga_ref, h_ref, wb_ref, wo_ref, gn_ref, h1_ref, hn_ref, *, tiles_per_seq):
    first_in_seq = pl.program_id(0) % tiles_per_seq == 0
    y_conv = _conv_branch_tile(cb_ref[...], cc_ref[...], ch_ref[...], ccp_ref[...], chp_ref[...],
                               cw_ref[...], first_in_seq)
    y_hg = _hgrn_norm_gate(oh_ref[...].astype(F32), hg_ref[...].astype(F32), ng_ref[...])
    merged = _sigmoid(gc_ref[...].astype(F32)) * _dot(y_conv.astype(BF16), wb_ref[0, 0])
    merged += _sigmoid(gh_ref[...].astype(F32)) * _dot(y_hg.astype(BF16), wb_ref[0, 1])
    merged += _sigmoid(ga_ref[...].astype(F32)) * _dot(ya_ref[...], wb_ref[0, 2])
    h1 = h_ref[...] + _dot(merged.astype(BF16), wo_ref[0])
    h1_ref[...] = h1
    hn_ref[...] = _rms(h1, gn_ref[...]).astype(BF16)


def mix_out(proj, o_hg, y_att, h, conv_w, norm_g, w_branch, w_o, layer, g_next, seq_len, tm):
    m, d = h.shape
    w = MIX_WIDTH
    assert seq_len % tm == 0 and tm % CONV_HALO == 0
    rows = lambda width: pl.BlockSpec((tm, width), lambda i: (i, 0))
    mix = lambda block: pl.BlockSpec((tm, w), lambda i: (i, _mix_block(proj, block)))
    halo = lambda block: pl.BlockSpec(
        (CONV_HALO, w), lambda i: (jnp.maximum(i * (tm // CONV_HALO) - 1, 0), _mix_block(proj, block)))
    gate = lambda branch: pl.BlockSpec((tm, d), lambda i: (i, branch))
    const = lambda i: (0, 0)
    resident = pl.Buffered(1)
    return pl.pallas_call(
        functools.partial(_mix_out_kernel, tiles_per_seq=seq_len // tm),
        grid=(m // tm,),
        in_specs=[mix(COL_CB), mix(COL_CC), mix(COL_CH), halo(COL_CC), halo(COL_CH),
                  pl.BlockSpec((CONV_K, w), const),
                  rows(w), mix(COL_HG), pl.BlockSpec((1, HEAD_DIM), const), rows(w),
                  gate(0), gate(1), gate(2), rows(d),
                  pl.BlockSpec((1,) + w_branch.shape[1:], lambda i: (layer, 0, 0, 0), pipeline_mode=resident),
                  pl.BlockSpec((1,) + w_o.shape[1:], lambda i: (layer, 0, 0), pipeline_mode=resident),
                  pl.BlockSpec((1, d), const)],
        out_specs=[rows(d), rows(d)],
        out_shape=[jax.ShapeDtypeStruct((m, d), F32), jax.ShapeDtypeStruct((m, d), BF16)],
        compiler_params=_params("parallel"),
        name="mix_out",
    )(proj, proj, proj, proj, proj, conv_w, o_hg, proj, norm_g.reshape(1, HEAD_DIM), y_att,
      proj, proj, proj, h, w_branch, w_o, g_next)


def _ffn_kernel(hn_ref, w1_ref, w2_ref, o_ref, acc_ref, *, n_steps):
    f = pl.program_id(1)
    last = n_steps - 1

    def part():
        hid = jnp.square(jnp.maximum(_dot(hn_ref[...], w1_ref[0, 0]), 0.0))
        return _dot(hid.astype(BF16), w2_ref[0])

    if n_steps == 1:
        o_ref[...] = part().astype(BF16)
        return

    @pl.when(f == 0)
    def _():
        acc_ref[...] = part()

    @pl.when((f > 0) & (f < last))
    def _():
        acc_ref[...] += part()

    @pl.when(f == last)
    def _():
        o_ref[...] = (acc_ref[...] + part()).astype(BF16)


def ffn(hn, w1, w2, layer, tm):
    m, d = hn.shape
    n_tiles, tf = w1.shape[1], w1.shape[3]
    return pl.pallas_call(
        functools.partial(_ffn_kernel, n_steps=n_tiles),
        grid=(m // tm, n_tiles),
        in_specs=[pl.BlockSpec((tm, d), lambda i, f: (i, 0)),
                  pl.BlockSpec((1, 1, d, tf), lambda i, f: (layer, f, 0, 0)),
                  pl.BlockSpec((1, tf, d), lambda i, f: (layer, f, 0))],
        out_specs=pl.BlockSpec((tm, d), lambda i, f: (i, 0)),
        out_shape=jax.ShapeDtypeStruct((m, d), BF16),
        scratch_shapes=[pltpu.VMEM((tm, d), F32)],
        compiler_params=_params("parallel", "arbitrary"),
        name="ffn",
    )(hn, w1, w2)


def _ple_kernel(h_ref, y_ref, g_ref, wg_ref, p_ref, wp_ref, gn_ref, *out_refs, last):
    x = h_ref[...] + y_ref[...].astype(F32)
    gate = _sigmoid(_dot(_rms(x, g_ref[...]).astype(BF16), wg_ref[0]))
    out = x + gate * _dot(p_ref[0].astype(BF16), wp_ref[0])
    if last:
        out_refs[0][...] = _rms(out, gn_ref[...])
    else:
        out_refs[0][...] = out
        out_refs[1][...] = _rms(out, gn_ref[...]).astype(BF16)


def ple_gate(h, y, g, w_gate, p, w_p, layer, g_next, last, tm):
    m, d = h.shape
    const = lambda i: (0, 0)
    slab = lambda i: (layer, 0, 0)
    rows = pl.BlockSpec((tm, d), lambda i: (i, 0))
    resident = pl.Buffered(1)
    if last:
        out_specs, out_shape = [rows], [jax.ShapeDtypeStruct((m, d), F32)]
    else:
        out_specs = [rows, rows]
        out_shape = [jax.ShapeDtypeStruct((m, d), F32), jax.ShapeDtypeStruct((m, d), BF16)]
    return pl.pallas_call(
        functools.partial(_ple_kernel, last=last),
        grid=(m // tm,),
        in_specs=[rows, rows,
                  pl.BlockSpec((1, d), const),
                  pl.BlockSpec((1,) + w_gate.shape[1:], slab, pipeline_mode=resident),
                  pl.BlockSpec((1, tm, p.shape[2]), lambda i: (layer, i, 0)),
                  pl.BlockSpec((1,) + w_p.shape[1:], slab, pipeline_mode=resident),
                  pl.BlockSpec((1, d), const)],
        out_specs=out_specs,
        out_shape=out_shape,
        compiler_params=_params("parallel"),
        name="ple_gate",
    )(h, y, g, w_gate, p, w_p, g_next)


def _tile(n, target):
    t = min(n, target)
    assert n % t == 0, (n, t)
    return t


def _tiles(m, d, dff):
    return {
        "proj_m_f32": _tile(m, 1024),
        "proj_m_bf16": _tile(m, 2048),
        "mix_m": _tile(m, 256),
        "ffn_m": _tile(m, 1024), "ffn_f": _tile(dff, 1024),
        "ple_m": _tile(m, 512),
    }


def kernel(x, p, w_in, conv_w, hg_lb_logits, hg_norm_g, att_rel_bias, w_branch, w_o,
           w_ff1, w_ff2, w_ple_in, w_ple_gate, g_mix, g_ff, g_ple, g_final):
    depth = w_in.shape[0]
    b, s, d = x.shape
    m = b * s
    w = MIX_WIDTH
    assert s % ATT_GQ == 0 and s % HG_BLOCK == 0 and d % w == 0
    assert w_in.shape[2] == (REF_FIRST_GATE_TILE + 3 * d // w) * w

    lb_sm = jax.nn.softmax(hg_lb_logits.astype(F32), axis=0)
    lb_all = jnp.cumsum(lb_sm, axis=0) - lb_sm[0]

    col_scale = jnp.ones((w_in.shape[2],), F32).at[REF_AQ_TILE * w:(REF_AQ_TILE + 1) * w].set(
        HEAD_DIM ** -0.5 * LOG2_E)
    t = _tiles(m, d, w_ff1.shape[2])
    col_tiles = lambda a, width: a.reshape(depth, a.shape[1], -1, width).transpose(0, 2, 1, 3)
    w_in_b = col_tiles((w_in * col_scale).astype(BF16), w)
    w_branch_b, w_o_b = w_branch.astype(BF16), w_o.astype(BF16)
    w_ff1_b, w_ff2_b = col_tiles(w_ff1.astype(BF16), t["ffn_f"]), w_ff2.astype(BF16)
    w_gate_b, w_ple_b = w_ple_gate.astype(BF16), w_ple_in.astype(BF16)
    p2 = p.reshape(depth, m, p.shape[3])

    h = x.reshape(m, d)
    xn = h
    for l in range(depth):
        last = l == depth - 1
        proj, z = in_proj(xn, g_mix[l].reshape(1, d), w_in_b, l,
                          tm=t["proj_m_bf16"] if xn.dtype == BF16 else t["proj_m_f32"])
        proj3 = proj.reshape(b, s, proj.shape[1])
        o_hg = hgrn_branch(proj3, z.reshape(b, s, w), lb_all[l], HG_HEADS_PER_STEP)
        y_att = attention_branch(proj3, att_rel_bias[l])
        h1, hn = mix_out(proj, o_hg.reshape(m, w), y_att.reshape(m, w), h, conv_w[l], hg_norm_g[l],
                         w_branch_b, w_o_b, l, g_ff[l].reshape(1, d), seq_len=s, tm=t["mix_m"])
        y = ffn(hn, w_ff1_b, w_ff2_b, l, tm=t["ffn_m"])
        g_next = (g_final if last else g_mix[l + 1]).reshape(1, d)
        outs = ple_gate(h1, y, g_ple[l].reshape(1, d), w_gate_b, p2, w_ple_b, l, g_next, last, tm=t["ple_m"])
        if last:
            h, = outs
        else:
            h, xn = outs
    return h.reshape(b, s, d)
```

```python
import functools

import numpy as np
import jax
import jax.numpy as jnp
from jax import lax
from jax.experimental import pallas as pl
from jax.experimental.pallas import tpu as pltpu

F32 = jnp.float32
BF16 = jnp.bfloat16

EPS = 1e-6
LB_FLOOR = 1e-30
MASK_VALUE = -1e30
LOG2_E = 1.4426950408889634

MIX_WIDTH = 1024
CONV_K = 3
HEADS = 8
HEAD_DIM = 128
CHUNK = 64
LEFT_CHUNKS = 8
MAX_REL_DIST = 256

HG_BLOCK = 128
HG_LEVELS = 7
ATT_GROUP_CHUNKS = 4
ATT_GQ = ATT_GROUP_CHUNKS * CHUNK
ATT_GK = (ATT_GROUP_CHUNKS + LEFT_CHUNKS) * CHUNK
ATT_HEADS_PER_STEP = 4
HG_HEADS_PER_STEP = 8
HG_UNROLL = 4

VMEM_LIMIT_BYTES = 58 * 1024 * 1024

COL_CB, COL_CC, COL_CH, COL_HQ, COL_HI, COL_HG, COL_AQ, COL_AK, COL_AV = range(9)
N_MIX_BLOCKS = 9
REF_HF_TILE = 4
REF_AQ_TILE = 7
REF_FIRST_GATE_TILE = 10


def _mix_block(proj, block):
    return proj.shape[-1] // MIX_WIDTH - N_MIX_BLOCKS + block


def _params(*sem):
    return pltpu.CompilerParams(dimension_semantics=sem, vmem_limit_bytes=VMEM_LIMIT_BYTES)


def _rms(x, g):
    ms = jnp.mean(x * x, axis=-1, keepdims=True)
    return x * lax.rsqrt(ms + EPS) * g


def _dot(a, b):
    return jnp.dot(a, b, preferred_element_type=F32)


def _dot_nt(a, b):
    return lax.dot_general(a, b, (((1,), (1,)), ((), ())), preferred_element_type=F32)


def _dot_tn(a, b):
    return lax.dot_general(a, b, (((0,), (0,)), ((), ())), preferred_element_type=F32)


def _sigmoid(x):
    return 1.0 / (1.0 + jnp.exp(-x))


def _in_proj_kernel(x_ref, g_ref, w_ref, ob_ref, of_ref, *scratch, normalize):
    j = pl.program_id(1)
    if normalize:
        xn_ref, = scratch

        @pl.when(j == 0)
        def _():
            xn_ref[...] = _rms(x_ref[...], g_ref[...]).astype(BF16)
    else:
        xn_ref = x_ref

    @pl.when(j != REF_HF_TILE)
    def _():
        ob_ref[...] = _dot(xn_ref[...], w_ref[0]).astype(BF16)

    @pl.when(j == REF_HF_TILE)
    def _():
        of_ref[...] = _dot(xn_ref[...], w_ref[0])


def in_proj(x, g, w, layer, tm):
    m, d = x.shape
    n = w.shape[2]
    tn = MIX_WIDTH
    n_gate_tiles = n // tn - REF_FIRST_GATE_TILE
    normalize = x.dtype != BF16

    def bf16_block(j):
        mixer = n_gate_tiles + jnp.where(j < REF_HF_TILE, j, jnp.maximum(j - 1, REF_HF_TILE - 1))
        return jnp.where(j >= REF_FIRST_GATE_TILE, j - REF_FIRST_GATE_TILE, mixer)

    return pl.pallas_call(
        functools.partial(_in_proj_kernel, normalize=normalize),
        grid=(m // tm, n // tn),
        in_specs=[
            pl.BlockSpec((tm, d), lambda i, j: (i, 0)),
            pl.BlockSpec((1, d), lambda i, j: (0, 0)),
            pl.BlockSpec((1, d, tn), lambda i, j: (layer, 0, j)),
        ],
        out_specs=[
            pl.BlockSpec((tm, tn), lambda i, j: (i, bf16_block(j))),
            pl.BlockSpec((tm, tn), lambda i, j: (i, 0)),
        ],
        out_shape=[
            jax.ShapeDtypeStruct((m, n - tn), BF16),
            jax.ShapeDtypeStruct((m, tn), F32),
        ],
        scratch_shapes=[pltpu.VMEM((tm, d), BF16)] if normalize else [],
        compiler_params=_params("parallel", "arbitrary"),
        name="in_proj",
    )(x, g, w)


def _hgrn_level_tables():
    t = np.arange(HG_BLOCK)
    x = t[:, None] ^ t[None, :]
    lvl = np.where(t[:, None] > t[None, :], np.floor(np.log2(np.maximum(x, 1))).astype(np.int64) + 1, -1)
    lvl = np.where(t[:, None] == t[None, :], 0, lvl).astype(np.int32)
    tri = (t[:, None] >= t[None, :]).astype(np.float32)
    return lvl, tri


def _split_point_rows(g, level):
    size, half = 1 << level, 1 << (level - 1)
    n = g.shape[0]
    if size >= 8:
        parts = [jnp.broadcast_to(g[b * size + half - 1:b * size + half, :], (size, g.shape[1]))
                 for b in range(n // size)]
        return jnp.concatenate(parts, axis=0)
    sub = lax.broadcasted_iota(jnp.int32, (8, g.shape[1]), 0)
    parts = []
    for v in range(n // 8):
        acc = None
        for b in range(8 // size):
            r = 8 * v + b * size + half - 1
            cand = jnp.broadcast_to(g[r:r + 1, :], (8, g.shape[1]))
            acc = cand if acc is None else jnp.where(sub >= b * size, cand, acc)
        parts.append(acc)
    return jnp.concatenate(parts, axis=0)


def _query_or_key_rows(q, key, level):
    size, half = 1 << level, 1 << (level - 1)
    n = q.shape[0]
    if half >= 8:
        parts = []
        for b in range(n // size):
            parts += [key[b * size:b * size + half], q[b * size + half:(b + 1) * size]]
        return jnp.concatenate(parts, axis=0)
    row = lax.broadcasted_iota(jnp.int32, q.shape, 0)
    return jnp.where((row & half) != 0, q, key)


def _neg_abs(x):
    bits = lax.bitcast_convert_type(x, jnp.uint32) | jnp.uint32(0x80000000)
    return lax.bitcast_convert_type(bits, F32)


def _hgrn_block(qs, vs, zs, lb_floors, one_m_lbs, tri, lvl, sts):
    L = HG_BLOCK
    heads = range(len(qs))
    fs, keys, g2s = [], [], []
    for h in heads:
        e = jnp.exp(_neg_abs(zs[h]))
        inv = 1.0 / (1.0 + e)
        einv = e * inv
        pos = zs[h] >= 0
        f = lb_floors[h] + one_m_lbs[h] * jnp.where(pos, inv, einv)
        fs.append(f)
        keys.append(one_m_lbs[h] * jnp.where(pos, einv, inv))
        lf2 = jnp.log2(f)
        hi = lf2.astype(BF16)
        lo = (lf2 - hi.astype(F32)).astype(BF16)
        c2 = _dot(tri, jnp.concatenate([hi, lo], axis=1))
        g2s.append(c2[:, :HEAD_DIM] + c2[:, HEAD_DIM:])

    scores = []
    for h in heads:
        p01 = _dot_nt(jnp.concatenate([qs[h], qs[h] * fs[h]], axis=0).astype(BF16), keys[h].astype(BF16))
        scores.append(jnp.where(lvl == 1, p01[L:], jnp.where(lvl == 0, p01[:L], 0.0)))
    for level in range(2, HG_LEVELS + 1):
        for h in heads:
            d = jnp.exp2(_neg_abs(g2s[h] - _split_point_rows(g2s[h], level)))
            x = (_query_or_key_rows(qs[h], keys[h], level) * d).astype(BF16)
            scores[h] = jnp.where(lvl == level, _dot_nt(x, x), scores[h])

    outs, new_sts = [], []
    for h in heads:
        g_tot = g2s[h][L - 1:L, :]
        outs.append(_dot(scores[h].astype(BF16), vs[h])
                    + _dot_nt((qs[h] * jnp.exp2(g2s[h])).astype(BF16), sts[h].astype(BF16)))
        kd = (keys[h] * jnp.exp2(g_tot - g2s[h])).astype(BF16)
        new_sts.append(sts[h] * jnp.exp2(g_tot) + _dot_tn(vs[h], kd))
    return outs, new_sts


def _hgrn_kernel(q_ref, v_ref, z_ref, lb_ref, tri_ref, lvl_ref, o_ref, st_ref):
    L = HG_BLOCK
    n_blocks = q_ref.shape[1] // L
    n_heads = q_ref.shape[2] // HEAD_DIM
    head_cols = [slice(hh * HEAD_DIM, (hh + 1) * HEAD_DIM) for hh in range(n_heads)]
    lb = lb_ref[0]
    lb_floors = [jnp.maximum(lb[:, c], LB_FLOOR) for c in head_cols]
    one_m_lbs = [1.0 - lb[:, c] for c in head_cols]
    tri = tri_ref[...]
    lvl = lvl_ref[...]
    st_ref[...] = jnp.zeros(st_ref.shape, F32)

    def body(n, carry):
        rows = pl.ds(pl.multiple_of(n * L, L), L)
        outs, new_sts = _hgrn_block(
            [q_ref[0, rows, c].astype(F32) for c in head_cols], [v_ref[0, rows, c] for c in head_cols],
            [z_ref[0, rows, c] for c in head_cols], lb_floors, one_m_lbs, tri, lvl,
            [st_ref[hh] for hh in range(n_heads)])
        for hh, c in enumerate(head_cols):
            st_ref[hh] = new_sts[hh]
            o_ref[0, rows, c] = outs[hh].astype(BF16)
        return carry

    lax.fori_loop(0, n_blocks, body, 0, unroll=HG_UNROLL)


def hgrn_branch(proj, z, lb, heads_per_step):
    b, s, _ = proj.shape
    lvl, tri = _hgrn_level_tables()
    width = heads_per_step * HEAD_DIM
    per = MIX_WIDTH // width

    def col(block):
        return pl.BlockSpec((1, s, width), lambda bi, h: (bi, 0, _mix_block(proj, block) * per + h))

    const = lambda bi, h: (0, 0)
    return pl.pallas_call(
        _hgrn_kernel,
        grid=(b, per),
        in_specs=[col(COL_HQ), col(COL_HI),
                  pl.BlockSpec((1, s, width), lambda bi, h: (bi, 0, h)),
                  pl.BlockSpec((1, 1, width), lambda bi, h: (h, 0, 0)),
                  pl.BlockSpec((HG_BLOCK, HG_BLOCK), const),
                  pl.BlockSpec((HG_BLOCK, HG_BLOCK), const)],
        out_specs=pl.BlockSpec((1, s, width), lambda bi, h: (bi, 0, h)),
        out_shape=jax.ShapeDtypeStruct((b, s, MIX_WIDTH), BF16),
        scratch_shapes=[pltpu.VMEM((heads_per_step, HEAD_DIM, HEAD_DIM), F32)],
        compiler_params=_params("parallel", "parallel"),
        name="hgrn_branch",
    )(proj, proj, z, lb.reshape(per, 1, width), jnp.asarray(tri, BF16), jnp.asarray(lvl))


def _att_group_bias(rel_table):
    heads = rel_table.shape[0]
    table = rel_table.astype(F32) * LOG2_E
    band = np.arange(ATT_GK)[None, :] // CHUNK - np.arange(ATT_GQ)[:, None] // CHUNK
    valid = (band >= 0) & (band <= LEFT_CHUNKS)
    lo, hi = LEFT_CHUNKS * CHUNK - (ATT_GK - 1), LEFT_CHUNKS * CHUNK + ATT_GQ - 1
    assert -MAX_REL_DIST <= lo and hi >= MAX_REL_DIST
    period = hi - lo + 2
    wr = jnp.concatenate([jnp.broadcast_to(table[:, 2 * MAX_REL_DIST:], (heads, hi - MAX_REL_DIST)),
                          jnp.flip(table[:, lo + MAX_REL_DIST:], axis=1),
                          jnp.zeros((heads, 1), F32)], axis=1)
    m = jnp.tile(wr, (1, ATT_GQ))[:, :ATT_GQ * (period - 1)].reshape(heads, ATT_GQ, period - 1)
    bias = m[:, :, ATT_GQ - 1:ATT_GQ - 1 + ATT_GK]
    return jnp.where(jnp.asarray(valid)[None], bias, MASK_VALUE)


def _att_kernel(q_ref, k_ref, v_ref, bias_ref, o_ref):
    n_groups = q_ref.shape[1] // ATT_GQ
    n_heads = q_ref.shape[2] // HEAD_DIM

    units = []
    for hh in range(n_heads):
        cols = slice(hh * HEAD_DIM, (hh + 1) * HEAD_DIM)
        for g in range(n_groups):
            q0 = g * ATT_GQ
            k0 = max(0, q0 + ATT_GQ - ATT_GK)
            units.append((hh, cols, q0, k0, q0 + ATT_GQ - k0))
    scores, probs = {}, {}
    for t in range(len(units) + 2):
        if t < len(units):
            hh, cols, q0, k0, nk = units[t]
            scores[t] = (_dot_nt(q_ref[0, q0:q0 + ATT_GQ, cols], k_ref[0, k0:k0 + nk, cols])
                         + bias_ref[hh, :, ATT_GK - nk:])
        if 0 <= t - 1 < len(units):
            s = scores.pop(t - 1)
            probs[t - 1] = jnp.exp2(s - jnp.max(s, axis=-1, keepdims=True)).astype(BF16)
        if 0 <= t - 2 < len(units):
            hh, cols, q0, k0, nk = units[t - 2]
            v = v_ref[0, k0:k0 + nk, cols]
            ov = _dot(probs.pop(t - 2), jnp.concatenate([v, jnp.ones_like(v)], axis=1))
            o = ov[:, :HEAD_DIM] / ov[:, HEAD_DIM:]
            o_ref[0, q0:q0 + ATT_GQ, cols] = o.astype(BF16)


def attention_branch(proj, rel_table):
    b, s, _ = proj.shape
    bias = _att_group_bias(rel_table)

    width = ATT_HEADS_PER_STEP * HEAD_DIM
    per = MIX_WIDTH // width

    def col(block):
        return pl.BlockSpec((1, s, width), lambda h, bi: (bi, 0, _mix_block(proj, block) * per + h))

    return pl.pallas_call(
        _att_kernel,
        grid=(per, b),
        in_specs=[col(COL_AQ), col(COL_AK), col(COL_AV),
                  pl.BlockSpec((ATT_HEADS_PER_STEP, ATT_GQ, ATT_GK), lambda h, bi: (h, 0, 0))],
        out_specs=pl.BlockSpec((1, s, width), lambda h, bi: (bi, 0, h)),
        out_shape=jax.ShapeDtypeStruct((b, s, MIX_WIDTH), BF16),
        compiler_params=_params("parallel", "parallel"),
        name="attention_branch",
    )(proj, proj, proj, bias)


CONV_HALO = 8


def _conv_branch_tile(cb, cc, ch, cc_prev, ch_prev, w, first_in_seq):
    u = cc.astype(F32) * ch.astype(F32)
    prev = jnp.where(first_in_seq, 0.0, cc_prev.astype(F32) * ch_prev.astype(F32))
    before1 = jnp.broadcast_to(prev[CONV_HALO - 1:CONV_HALO, :], u.shape)
    before2 = jnp.broadcast_to(prev[CONV_HALO - 2:CONV_HALO - 1, :], u.shape)
    row = lax.broadcasted_iota(jnp.int32, u.shape, 0)
    u1 = jnp.where(row >= 1, pltpu.roll(u, 1, 0), before1)
    u2 = jnp.where(row >= 2, pltpu.roll(u, 2, 0), jnp.where(row == 1, before1, before2))
    return cb.astype(F32) * (u2 * w[0:1, :] + u1 * w[1:2, :] + u * w[2:3, :])


def _hgrn_norm_gate(o, gate, ng):
    parts = []
    for hh in range(o.shape[1] // HEAD_DIM):
        cols = slice(hh * HEAD_DIM, (hh + 1) * HEAD_DIM)
        g = gate[:, cols]
        parts.append(_rms(o[:, cols], ng) * (g * _sigmoid(g)))
    return jnp.concatenate(parts, axis=1)


def _mix_out_kernel(cb_ref, cc_ref, ch_ref, ccp_ref, chp_ref, cw_ref, oh_ref, hg_ref, ng_ref, ya_ref,
                    gc_ref, gh_ref, ga_ref, h_ref, wb_ref, wo_ref, gn_ref, h1_ref, hn_ref, *, tiles_per_seq):
    first_in_seq = pl.program_id(0) % tiles_per_seq == 0
    y_conv = _conv_branch_tile(cb_ref[...], cc_ref[...], ch_ref[...], ccp_ref[...], chp_ref[...],
                               cw_ref[...], first_in_seq)
    y_hg = _hgrn_norm_gate(oh_ref[...].astype(F32), hg_ref[...].astype(F32), ng_ref[...])
    merged = _sigmoid(gc_ref[...].astype(F32)) * _dot(y_conv.astype(BF16), wb_ref[0, 0])
    merged += _sigmoid(gh_ref[...].astype(F32)) * _dot(y_hg.astype(BF16), wb_ref[0, 1])
    merged += _sigmoid(ga_ref[...].astype(F32)) * _dot(ya_ref[...], wb_ref[0, 2])
    h1 = h_ref[...] + _dot(merged.astype(BF16), wo_ref[0])
    h1_ref[...] = h1
    hn_ref[...] = _rms(h1, gn_ref[...]).astype(BF16)


def mix_out(proj, o_hg, y_att, h, conv_w, norm_g, w_branch, w_o, layer, g_next, seq_len, tm):
    m, d = h.shape
    w = MIX_WIDTH
    assert seq_len % tm == 0 and tm % CONV_HALO == 0
    rows = lambda width: pl.BlockSpec((tm, width), lambda i: (i, 0))
    mix = lambda block: pl.BlockSpec((tm, w), lambda i: (i, _mix_block(proj, block)))
    halo = lambda block: pl.BlockSpec(
        (CONV_HALO, w), lambda i: (jnp.maximum(i * (tm // CONV_HALO) - 1, 0), _mix_block(proj, block)))
    gate = lambda branch: pl.BlockSpec((tm, d), lambda i: (i, branch))
    const = lambda i: (0, 0)
    resident = pl.Buffered(1)
    return pl.pallas_call(
        functools.partial(_mix_out_kernel, tiles_per_seq=seq_len // tm),
        grid=(m // tm,),
        in_specs=[mix(COL_CB), mix(COL_CC), mix(COL_CH), halo(COL_CC), halo(COL_CH),
                  pl.BlockSpec((CONV_K, w), const),
                  rows(w), mix(COL_HG), pl.BlockSpec((1, HEAD_DIM), const), rows(w),
                  gate(0), gate(1), gate(2), rows(d),
                  pl.BlockSpec((1,) + w_branch.shape[1:], lambda i: (layer, 0, 0, 0), pipeline_mode=resident),
                  pl.BlockSpec((1,) + w_o.shape[1:], lambda i: (layer, 0, 0), pipeline_mode=resident),
                  pl.BlockSpec((1, d), const)],
        out_specs=[rows(d), rows(d)],
        out_shape=[jax.ShapeDtypeStruct((m, d), F32), jax.ShapeDtypeStruct((m, d), BF16)],
        compiler_params=_params("parallel"),
        name="mix_out",
    )(proj, proj, proj, proj, proj, conv_w, o_hg, proj, norm_g.reshape(1, HEAD_DIM), y_att,
      proj, proj, proj, h, w_branch, w_o, g_next)


def _ffn_kernel(hn_ref, w1_ref, w2_ref, o_ref, acc_ref, *, n_steps):
    f = pl.program_id(1)
    last = n_steps - 1

    def part():
        hid = jnp.square(jnp.maximum(_dot(hn_ref[...], w1_ref[0]), 0.0))
        return _dot(hid.astype(BF16), w2_ref[0])

    if n_steps == 1:
        o_ref[...] = part().astype(BF16)
        return

    @pl.when(f == 0)
    def _():
        acc_ref[...] = part()

    @pl.when((f > 0) & (f < last))
    def _():
        acc_ref[...] += part()

    @pl.when(f == last)
    def _():
        o_ref[...] = (acc_ref[...] + part()).astype(BF16)


def ffn(hn, w1, w2, layer, tm, tf):
    m, d = hn.shape
    dff = w1.shape[2]
    return pl.pallas_call(
        functools.partial(_ffn_kernel, n_steps=dff // tf),
        grid=(m // tm, dff // tf),
        in_specs=[pl.BlockSpec((tm, d), lambda i, f: (i, 0)),
                  pl.BlockSpec((1, d, tf), lambda i, f: (layer, 0, f)),
                  pl.BlockSpec((1, tf, d), lambda i, f: (layer, f, 0))],
        out_specs=pl.BlockSpec((tm, d), lambda i, f: (i, 0)),
        out_shape=jax.ShapeDtypeStruct((m, d), BF16),
        scratch_shapes=[pltpu.VMEM((tm, d), F32)],
        compiler_params=_params("parallel", "arbitrary"),
        name="ffn",
    )(hn, w1, w2)


def _ple_kernel(h_ref, y_ref, g_ref, wg_ref, p_ref, wp_ref, gn_ref, *out_refs, last):
    x = h_ref[...] + y_ref[...].astype(F32)
    gate = _sigmoid(_dot(_rms(x, g_ref[...]).astype(BF16), wg_ref[0]))
    out = x + gate * _dot(p_ref[0].astype(BF16), wp_ref[0])
    if last:
        out_refs[0][...] = _rms(out, gn_ref[...])
    else:
        out_refs[0][...] = out
        out_refs[1][...] = _rms(out, gn_ref[...]).astype(BF16)


def ple_gate(h, y, g, w_gate, p, w_p, layer, g_next, last, tm):
    m, d = h.shape
    const = lambda i: (0, 0)
    slab = lambda i: (layer, 0, 0)
    rows = pl.BlockSpec((tm, d), lambda i: (i, 0))
    resident = pl.Buffered(1)
    if last:
        out_specs, out_shape = [rows], [jax.ShapeDtypeStruct((m, d), F32)]
    else:
        out_specs = [rows, rows]
        out_shape = [jax.ShapeDtypeStruct((m, d), F32), jax.ShapeDtypeStruct((m, d), BF16)]
    return pl.pallas_call(
        functools.partial(_ple_kernel, last=last),
        grid=(m // tm,),
        in_specs=[rows, rows,
                  pl.BlockSpec((1, d), const),
                  pl.BlockSpec((1,) + w_gate.shape[1:], slab, pipeline_mode=resident),
                  pl.BlockSpec((1, tm, p.shape[2]), lambda i: (layer, i, 0)),
                  pl.BlockSpec((1,) + w_p.shape[1:], slab, pipeline_mode=resident),
                  pl.BlockSpec((1, d), const)],
        out_specs=out_specs,
        out_shape=out_shape,
        compiler_params=_params("parallel"),
        name="ple_gate",
    )(h, y, g, w_gate, p, w_p, g_next)


def _tile(n, target):
    t = min(n, target)
    assert n % t == 0, (n, t)
    return t


def _tiles(m, d, dff):
    return {
        "proj_m": _tile(m, 1024),
        "mix_m": _tile(m, 256),
        "ffn_m": _tile(m, 1024), "ffn_f": _tile(dff, 1024),
        "ple_m": _tile(m, 512),
    }


def kernel(x, p, w_in, conv_w, hg_lb_logits, hg_norm_g, att_rel_bias, w_branch, w_o,
           w_ff1, w_ff2, w_ple_in, w_ple_gate, g_mix, g_ff, g_ple, g_final):
    depth = w_in.shape[0]
    b, s, d = x.shape
    m = b * s
    w = MIX_WIDTH
    assert s % ATT_GQ == 0 and s % HG_BLOCK == 0 and d % w == 0
    assert w_in.shape[2] == (REF_FIRST_GATE_TILE + 3 * d // w) * w

    lb_sm = jax.nn.softmax(hg_lb_logits.astype(F32), axis=0)
    lb_all = jnp.cumsum(lb_sm, axis=0) - lb_sm[0]

    col_scale = jnp.ones((w_in.shape[2],), F32).at[REF_AQ_TILE * w:(REF_AQ_TILE + 1) * w].set(
        HEAD_DIM ** -0.5 * LOG2_E)
    w_in_b = (w_in * col_scale).astype(BF16)
    w_branch_b, w_o_b = w_branch.astype(BF16), w_o.astype(BF16)
    w_ff1_b, w_ff2_b = w_ff1.astype(BF16), w_ff2.astype(BF16)
    w_gate_b, w_ple_b = w_ple_gate.astype(BF16), w_ple_in.astype(BF16)
    p2 = p.reshape(depth, m, p.shape[3])

    t = _tiles(m, d, w_ff1.shape[2])
    h = x.reshape(m, d)
    xn = h
    for l in range(depth):
        last = l == depth - 1
        proj, z = in_proj(xn, g_mix[l].reshape(1, d), w_in_b, l, tm=t["proj_m"])
        proj3 = proj.reshape(b, s, proj.shape[1])
        o_hg = hgrn_branch(proj3, z.reshape(b, s, w), lb_all[l], HG_HEADS_PER_STEP)
        y_att = attention_branch(proj3, att_rel_bias[l])
        h1, hn = mix_out(proj, o_hg.reshape(m, w), y_att.reshape(m, w), h, conv_w[l], hg_norm_g[l],
                         w_branch_b, w_o_b, l, g_ff[l].reshape(1, d), seq_len=s, tm=t["mix_m"])
        y = ffn(hn, w_ff1_b, w_ff2_b, l, tm=t["ffn_m"], tf=t["ffn_f"])
        g_next = (g_final if last else g_mix[l + 1]).reshape(1, d)
        outs = ple_gate(h1, y, g_ple[l].reshape(1, d), w_gate_b, p2, w_ple_b, l, g_next, last, tm=t["ple_m"])
        if last:
            h, = outs
        else:
            h, xn = outs
    return h.reshape(b, s, d)
```

```python
import functools

import numpy as np
import jax
import jax.numpy as jnp
from jax import lax
from jax.experimental import pallas as pl
from jax.experimental.pallas import tpu as pltpu

F32 = jnp.float32
BF16 = jnp.bfloat16

EPS = 1e-6
LB_FLOOR = 1e-30
MASK_VALUE = -1e30
LOG2_E = 1.4426950408889634

MIX_WIDTH = 1024
CONV_K = 3
HEADS = 8
HEAD_DIM = 128
CHUNK = 64
LEFT_CHUNKS = 8
MAX_REL_DIST = 256

HG_BLOCK = 128
HG_LEVELS = 7
ATT_GROUP_CHUNKS = 4
ATT_GQ = ATT_GROUP_CHUNKS * CHUNK
ATT_GK = (ATT_GROUP_CHUNKS + LEFT_CHUNKS) * CHUNK
ATT_HEADS_PER_STEP = 4
HG_HEADS_PER_STEP = 8
HG_UNROLL = 4

VMEM_LIMIT_BYTES = 58 * 1024 * 1024

COL_CB, COL_CC, COL_CH, COL_HQ, COL_HI, COL_HG, COL_AQ, COL_AK, COL_AV = range(9)
N_MIX_BLOCKS = 9
REF_HF_TILE = 4
REF_AQ_TILE = 7
REF_FIRST_GATE_TILE = 10


def _mix_block(proj, block):
    return proj.shape[-1] // MIX_WIDTH - N_MIX_BLOCKS + block


def _params(*sem):
    return pltpu.CompilerParams(dimension_semantics=sem, vmem_limit_bytes=VMEM_LIMIT_BYTES)


def _rms(x, g):
    ms = jnp.mean(x * x, axis=-1, keepdims=True)
    return x * lax.rsqrt(ms + EPS) * g


def _dot(a, b):
    return jnp.dot(a, b, preferred_element_type=F32)


def _dot_nt(a, b):
    return lax.dot_general(a, b, (((1,), (1,)), ((), ())), preferred_element_type=F32)


def _dot_tn(a, b):
    return lax.dot_general(a, b, (((0,), (0,)), ((), ())), preferred_element_type=F32)


def _sigmoid(x):
    return 1.0 / (1.0 + jnp.exp(-x))


def _in_proj_kernel(x_ref, g_ref, w_ref, ob_ref, of_ref, *scratch, normalize):
    j = pl.program_id(1 if normalize else 0)
    if normalize:
        xn_ref, = scratch

        @pl.when(j == 0)
        def _():
            xn_ref[...] = _rms(x_ref[...], g_ref[...]).astype(BF16)
    else:
        xn_ref = x_ref

    @pl.when(j != REF_HF_TILE)
    def _():
        ob_ref[...] = _dot(xn_ref[...], w_ref[0]).astype(BF16)

    @pl.when(j == REF_HF_TILE)
    def _():
        of_ref[...] = _dot(xn_ref[...], w_ref[0])


def in_proj(x, g, w, layer, tm):
    m, d = x.shape
    n = w.shape[2]
    tn = MIX_WIDTH
    n_gate_tiles = n // tn - REF_FIRST_GATE_TILE
    normalize = x.dtype != BF16
    last_i = m // tm - 1
    assert REF_HF_TILE >= 1

    def bf16_col(j):
        mixer = n_gate_tiles + jnp.where(j < REF_HF_TILE, j, jnp.maximum(j - 1, REF_HF_TILE - 1))
        return jnp.where(j >= REF_FIRST_GATE_TILE, j - REF_FIRST_GATE_TILE, mixer)

    if normalize:
        grid = (m // tm, n // tn)
        ij = lambda a, b: (a, b)
        bf16_block = lambda i, j: (i, bf16_col(j))
        f32_block = lambda i, j: (i, 0)
    else:
        grid = (n // tn, m // tm)
        ij = lambda a, b: (b, a)
        bf16_block = lambda i, j: (jnp.where(j == REF_HF_TILE, last_i, i), bf16_col(j))
        f32_block = lambda i, j: (jnp.where(j < REF_HF_TILE, 0, jnp.where(j == REF_HF_TILE, i, last_i)), 0)

    return pl.pallas_call(
        functools.partial(_in_proj_kernel, normalize=normalize),
        grid=grid,
        in_specs=[
            pl.BlockSpec((tm, d), lambda a, b: (ij(a, b)[0], 0)),
            pl.BlockSpec((1, d), lambda a, b: (0, 0)),
            pl.BlockSpec((1, d, tn), lambda a, b: (layer, 0, ij(a, b)[1])),
        ],
        out_specs=[
            pl.BlockSpec((tm, tn), lambda a, b: bf16_block(*ij(a, b))),
            pl.BlockSpec((tm, tn), lambda a, b: f32_block(*ij(a, b))),
        ],
        out_shape=[
            jax.ShapeDtypeStruct((m, n - tn), BF16),
            jax.ShapeDtypeStruct((m, tn), F32),
        ],
        scratch_shapes=[pltpu.VMEM((tm, d), BF16)] if normalize else [],
        compiler_params=_params("parallel", "arbitrary") if normalize else _params("arbitrary", "arbitrary"),
        name="in_proj",
    )(x, g, w)


def _hgrn_level_tables():
    t = np.arange(HG_BLOCK)
    x = t[:, None] ^ t[None, :]
    lvl = np.where(t[:, None] > t[None, :], np.floor(np.log2(np.maximum(x, 1))).astype(np.int64) + 1, -1)
    lvl = np.where(t[:, None] == t[None, :], 0, lvl).astype(np.int32)
    tri = (t[:, None] >= t[None, :]).astype(np.float32)
    return lvl, tri


def _split_point_rows(g, level):
    size, half = 1 << level, 1 << (level - 1)
    n = g.shape[0]
    if size >= 8:
        parts = [jnp.broadcast_to(g[b * size + half - 1:b * size + half, :], (size, g.shape[1]))
                 for b in range(n // size)]
        return jnp.concatenate(parts, axis=0)
    sub = lax.broadcasted_iota(jnp.int32, (8, g.shape[1]), 0)
    parts = []
    for v in range(n // 8):
        acc = None
        for b in range(8 // size):
            r = 8 * v + b * size + half - 1
            cand = jnp.broadcast_to(g[r:r + 1, :], (8, g.shape[1]))
            acc = cand if acc is None else jnp.where(sub >= b * size, cand, acc)
        parts.append(acc)
    return jnp.concatenate(parts, axis=0)


def _query_or_key_rows(q, key, level):
    size, half = 1 << level, 1 << (level - 1)
    n = q.shape[0]
    if half >= 8:
        parts = []
        for b in range(n // size):
            parts += [key[b * size:b * size + half], q[b * size + half:(b + 1) * size]]
        return jnp.concatenate(parts, axis=0)
    row = lax.broadcasted_iota(jnp.int32, q.shape, 0)
    return jnp.where((row & half) != 0, q, key)


def _neg_abs(x):
    bits = lax.bitcast_convert_type(x, jnp.uint32) | jnp.uint32(0x80000000)
    return lax.bitcast_convert_type(bits, F32)


def _hgrn_block(qs, vs, zs, lb_floors, one_m_lbs, tri, lvl, sts):
    L = HG_BLOCK
    heads = range(len(qs))
    fs, keys, g2s = [], [], []
    for h in heads:
        e = jnp.exp(_neg_abs(zs[h]))
        inv = 1.0 / (1.0 + e)
        einv = e * inv
        pos = zs[h] >= 0
        f = lb_floors[h] + one_m_lbs[h] * jnp.where(pos, inv, einv)
        fs.append(f)
        keys.append(one_m_lbs[h] * jnp.where(pos, einv, inv))
        lf2 = jnp.log2(f)
        hi = lf2.astype(BF16)
        lo = (lf2 - hi.astype(F32)).astype(BF16)
        c2 = _dot(tri, jnp.concatenate([hi, lo], axis=1))
        g2s.append(c2[:, :HEAD_DIM] + c2[:, HEAD_DIM:])

    scores = []
    for h in heads:
        p01 = _dot_nt(jnp.concatenate([qs[h], qs[h] * fs[h]], axis=0).astype(BF16), keys[h].astype(BF16))
        scores.append(jnp.where(lvl == 1, p01[L:], jnp.where(lvl == 0, p01[:L], 0.0)))
    for level in range(2, HG_LEVELS + 1):
        for h in heads:
            d = jnp.exp2(_neg_abs(g2s[h] - _split_point_rows(g2s[h], level)))
            x = (_query_or_key_rows(qs[h], keys[h], level) * d).astype(BF16)
            scores[h] = jnp.where(lvl == level, _dot_nt(x, x), scores[h])

    outs, new_sts = [], []
    for h in heads:
        g_tot = g2s[h][L - 1:L, :]
        outs.append(_dot(scores[h].astype(BF16), vs[h])
                    + _dot_nt((qs[h] * jnp.exp2(g2s[h])).astype(BF16), sts[h].astype(BF16)))
        kd = (keys[h] * jnp.exp2(g_tot - g2s[h])).astype(BF16)
        new_sts.append(sts[h] * jnp.exp2(g_tot) + _dot_tn(vs[h], kd))
    return outs, new_sts


def _hgrn_kernel(q_ref, v_ref, z_ref, lb_ref, tri_ref, lvl_ref, o_ref, st_ref):
    L = HG_BLOCK
    n_blocks = q_ref.shape[1] // L
    n_heads = q_ref.shape[2] // HEAD_DIM
    head_cols = [slice(hh * HEAD_DIM, (hh + 1) * HEAD_DIM) for hh in range(n_heads)]
    lb = lb_ref[0]
    lb_floors = [jnp.maximum(lb[:, c], LB_FLOOR) for c in head_cols]
    one_m_lbs = [1.0 - lb[:, c] for c in head_cols]
    tri = tri_ref[...]
    lvl = lvl_ref[...]
    st_ref[...] = jnp.zeros(st_ref.shape, F32)

    def body(n, carry):
        rows = pl.ds(pl.multiple_of(n * L, L), L)
        outs, new_sts = _hgrn_block(
            [q_ref[0, rows, c].astype(F32) for c in head_cols], [v_ref[0, rows, c] for c in head_cols],
            [z_ref[0, rows, c] for c in head_cols], lb_floors, one_m_lbs, tri, lvl,
            [st_ref[hh] for hh in range(n_heads)])
        for hh, c in enumerate(head_cols):
            st_ref[hh] = new_sts[hh]
            o_ref[0, rows, c] = outs[hh].astype(BF16)
        return carry

    lax.fori_loop(0, n_blocks, body, 0, unroll=HG_UNROLL)


def hgrn_branch(proj, z, lb, heads_per_step):
    b, s, _ = proj.shape
    lvl, tri = _hgrn_level_tables()
    width = heads_per_step * HEAD_DIM
    per = MIX_WIDTH // width

    def col(block):
        return pl.BlockSpec((1, s, width), lambda bi, h: (bi, 0, _mix_block(proj, block) * per + h))

    const = lambda bi, h: (0, 0)
    return pl.pallas_call(
        _hgrn_kernel,
        grid=(b, per),
        in_specs=[col(COL_HQ), col(COL_HI),
                  pl.BlockSpec((1, s, width), lambda bi, h: (bi, 0, h)),
                  pl.BlockSpec((1, 1, width), lambda bi, h: (h, 0, 0)),
                  pl.BlockSpec((HG_BLOCK, HG_BLOCK), const),
                  pl.BlockSpec((HG_BLOCK, HG_BLOCK), const)],
        out_specs=pl.BlockSpec((1, s, width), lambda bi, h: (bi, 0, h)),
        out_shape=jax.ShapeDtypeStruct((b, s, MIX_WIDTH), BF16),
        scratch_shapes=[pltpu.VMEM((heads_per_step, HEAD_DIM, HEAD_DIM), F32)],
        compiler_params=_params("parallel", "parallel"),
        name="hgrn_branch",
    )(proj, proj, z, lb.reshape(per, 1, width), jnp.asarray(tri, BF16), jnp.asarray(lvl))


def _att_group_bias(rel_table):
    heads = rel_table.shape[0]
    table = rel_table.astype(F32) * LOG2_E
    band = np.arange(ATT_GK)[None, :] // CHUNK - np.arange(ATT_GQ)[:, None] // CHUNK
    valid = (band >= 0) & (band <= LEFT_CHUNKS)
    lo, hi = LEFT_CHUNKS * CHUNK - (ATT_GK - 1), LEFT_CHUNKS * CHUNK + ATT_GQ - 1
    assert -MAX_REL_DIST <= lo and hi >= MAX_REL_DIST
    period = hi - lo + 2
    wr = jnp.concatenate([jnp.broadcast_to(table[:, 2 * MAX_REL_DIST:], (heads, hi - MAX_REL_DIST)),
                          jnp.flip(table[:, lo + MAX_REL_DIST:], axis=1),
                          jnp.zeros((heads, 1), F32)], axis=1)
    m = jnp.tile(wr, (1, ATT_GQ))[:, :ATT_GQ * (period - 1)].reshape(heads, ATT_GQ, period - 1)
    bias = m[:, :, ATT_GQ - 1:ATT_GQ - 1 + ATT_GK]
    return jnp.where(jnp.asarray(valid)[None], bias, MASK_VALUE)


def _att_kernel(q_ref, k_ref, v_ref, bias_ref, o_ref):
    n_groups = q_ref.shape[1] // ATT_GQ
    n_heads = q_ref.shape[2] // HEAD_DIM

    units = []
    for hh in range(n_heads):
        cols = slice(hh * HEAD_DIM, (hh + 1) * HEAD_DIM)
        for g in range(n_groups):
            q0 = g * ATT_GQ
            k0 = max(0, q0 + ATT_GQ - ATT_GK)
            units.append((hh, cols, q0, k0, q0 + ATT_GQ - k0))
    scores, probs = {}, {}
    for t in range(len(units) + 2):
        if t < len(units):
            hh, cols, q0, k0, nk = units[t]
            scores[t] = (_dot_nt(q_ref[0, q0:q0 + ATT_GQ, cols], k_ref[0, k0:k0 + nk, cols])
                         + bias_ref[hh, :, ATT_GK - nk:])
        if 0 <= t - 1 < len(units):
            s = scores.pop(t - 1)
            probs[t - 1] = jnp.exp2(s - jnp.max(s, axis=-1, keepdims=True)).astype(BF16)
        if 0 <= t - 2 < len(units):
            hh, cols, q0, k0, nk = units[t - 2]
            v = v_ref[0, k0:k0 + nk, cols]
            ov = _dot(probs.pop(t - 2), jnp.concatenate([v, jnp.ones_like(v)], axis=1))
            o = ov[:, :HEAD_DIM] / ov[:, HEAD_DIM:]
            o_ref[0, q0:q0 + ATT_GQ, cols] = o.astype(BF16)


def attention_branch(proj, rel_table):
    b, s, _ = proj.shape
    bias = _att_group_bias(rel_table)

    width = ATT_HEADS_PER_STEP * HEAD_DIM
    per = MIX_WIDTH // width

    def col(block):
        return pl.BlockSpec((1, s, width), lambda h, bi: (bi, 0, _mix_block(proj, block) * per + h))

    return pl.pallas_call(
        _att_kernel,
        grid=(per, b),
        in_specs=[col(COL_AQ), col(COL_AK), col(COL_AV),
                  pl.BlockSpec((ATT_HEADS_PER_STEP, ATT_GQ, ATT_GK), lambda h, bi: (h, 0, 0))],
        out_specs=pl.BlockSpec((1, s, width), lambda h, bi: (bi, 0, h)),
        out_shape=jax.ShapeDtypeStruct((b, s, MIX_WIDTH), BF16),
        compiler_params=_params("parallel", "parallel"),
        name="attention_branch",
    )(proj, proj, proj, bias)


CONV_HALO = 8


def _conv_branch_tile(cb, cc, ch, cc_prev, ch_prev, w, first_in_seq):
    u = cc.astype(F32) * ch.astype(F32)
    prev = jnp.where(first_in_seq, 0.0, cc_prev.astype(F32) * ch_prev.astype(F32))
    before1 = jnp.broadcast_to(prev[CONV_HALO - 1:CONV_HALO, :], u.shape)
    before2 = jnp.broadcast_to(prev[CONV_HALO - 2:CONV_HALO - 1, :], u.shape)
    row = lax.broadcasted_iota(jnp.int32, u.shape, 0)
    u1 = jnp.where(row >= 1, pltpu.roll(u, 1, 0), before1)
    u2 = jnp.where(row >= 2, pltpu.roll(u, 2, 0), jnp.where(row == 1, before1, before2))
    return cb.astype(F32) * (u2 * w[0:1, :] + u1 * w[1:2, :] + u * w[2:3, :])


def _hgrn_norm_gate(o, gate, ng):
    parts = []
    for hh in range(o.shape[1] // HEAD_DIM):
        cols = slice(hh * HEAD_DIM, (hh + 1) * HEAD_DIM)
        g = gate[:, cols]
        parts.append(_rms(o[:, cols], ng) * (g * _sigmoid(g)))
    return jnp.concatenate(parts, axis=1)


def _mix_out_kernel(cb_ref, cc_ref, ch_ref, ccp_ref, chp_ref, cw_ref, oh_ref, hg_ref, ng_ref, ya_ref,
                    gc_ref, gh_ref, ga_ref, h_ref, wb_ref, wo_ref, gn_ref, h1_ref, hn_ref, *, tiles_per_seq):
    first_in_seq = pl.program_id(0) % tiles_per_seq == 0
    y_conv = _conv_branch_tile(cb_ref[...], cc_ref[...], ch_ref[...], ccp_ref[...], chp_ref[...],
                               cw_ref[...], first_in_seq)
    y_hg = _hgrn_norm_gate(oh_ref[...].astype(F32), hg_ref[...].astype(F32), ng_ref[...])
    merged = _sigmoid(gc_ref[...].astype(F32)) * _dot(y_conv.astype(BF16), wb_ref[0, 0])
    merged += _sigmoid(gh_ref[...].astype(F32)) * _dot(y_hg.astype(BF16), wb_ref[0, 1])
    merged += _sigmoid(ga_ref[...].astype(F32)) * _dot(ya_ref[...], wb_ref[0, 2])
    h1 = h_ref[...] + _dot(merged.astype(BF16), wo_ref[0])
    h1_ref[...] = h1
    hn_ref[...] = _rms(h1, gn_ref[...]).astype(BF16)


def mix_out(proj, o_hg, y_att, h, conv_w, norm_g, w_branch, w_o, layer, g_next, seq_len, tm):
    m, d = h.shape
    w = MIX_WIDTH
    assert seq_len % tm == 0 and tm % CONV_HALO == 0
    rows = lambda width: pl.BlockSpec((tm, width), lambda i: (i, 0))
    mix = lambda block: pl.BlockSpec((tm, w), lambda i: (i, _mix_block(proj, block)))
    halo = lambda block: pl.BlockSpec(
        (CONV_HALO, w), lambda i: (jnp.maximum(i * (tm // CONV_HALO) - 1, 0), _mix_block(proj, block)))
    gate = lambda branch: pl.BlockSpec((tm, d), lambda i: (i, branch))
    const = lambda i: (0, 0)
    resident = pl.Buffered(1)
    return pl.pallas_call(
        functools.partial(_mix_out_kernel, tiles_per_seq=seq_len // tm),
        grid=(m // tm,),
        in_specs=[mix(COL_CB), mix(COL_CC), mix(COL_CH), halo(COL_CC), halo(COL_CH),
                  pl.BlockSpec((CONV_K, w), const),
                  rows(w), mix(COL_HG), pl.BlockSpec((1, HEAD_DIM), const), rows(w),
                  gate(0), gate(1), gate(2), rows(d),
                  pl.BlockSpec((1,) + w_branch.shape[1:], lambda i: (layer, 0, 0, 0), pipeline_mode=resident),
                  pl.BlockSpec((1,) + w_o.shape[1:], lambda i: (layer, 0, 0), pipeline_mode=resident),
                  pl.BlockSpec((1, d), const)],
        out_specs=[rows(d), rows(d)],
        out_shape=[jax.ShapeDtypeStruct((m, d), F32), jax.ShapeDtypeStruct((m, d), BF16)],
        compiler_params=_params("parallel"),
        name="mix_out",
    )(proj, proj, proj, proj, proj, conv_w, o_hg, proj, norm_g.reshape(1, HEAD_DIM), y_att,
      proj, proj, proj, h, w_branch, w_o, g_next)


def _ffn_kernel(hn_ref, w1_ref, w2_ref, o_ref, acc_ref, *, n_steps):
    f = pl.program_id(1)
    last = n_steps - 1

    def part():
        hid = jnp.square(jnp.maximum(_dot(hn_ref[...], w1_ref[0]), 0.0))
        return _dot(hid.astype(BF16), w2_ref[0])

    if n_steps == 1:
        o_ref[...] = part().astype(BF16)
        return

    @pl.when(f == 0)
    def _():
        acc_ref[...] = part()

    @pl.when((f > 0) & (f < last))
    def _():
        acc_ref[...] += part()

    @pl.when(f == last)
    def _():
        o_ref[...] = (acc_ref[...] + part()).astype(BF16)


def ffn(hn, w1, w2, layer, tm, tf):
    m, d = hn.shape
    dff = w1.shape[2]
    return pl.pallas_call(
        functools.partial(_ffn_kernel, n_steps=dff // tf),
        grid=(m // tm, dff // tf),
        in_specs=[pl.BlockSpec((tm, d), lambda i, f: (i, 0)),
                  pl.BlockSpec((1, d, tf), lambda i, f: (layer, 0, f)),
                  pl.BlockSpec((1, tf, d), lambda i, f: (layer, f, 0))],
        out_specs=pl.BlockSpec((tm, d), lambda i, f: (i, 0)),
        out_shape=jax.ShapeDtypeStruct((m, d), BF16),
        scratch_shapes=[pltpu.VMEM((tm, d), F32)],
        compiler_params=_params("parallel", "arbitrary"),
        name="ffn",
    )(hn, w1, w2)


def _ple_kernel(h_ref, y_ref, g_ref, wg_ref, p_ref, wp_ref, gn_ref, *out_refs, last):
    x = h_ref[...] + y_ref[...].astype(F32)
    gate = _sigmoid(_dot(_rms(x, g_ref[...]).astype(BF16), wg_ref[0]))
    out = x + gate * _dot(p_ref[0].astype(BF16), wp_ref[0])
    if last:
        out_refs[0][...] = _rms(out, gn_ref[...])
    else:
        out_refs[0][...] = out
        out_refs[1][...] = _rms(out, gn_ref[...]).astype(BF16)


def ple_gate(h, y, g, w_gate, p, w_p, layer, g_next, last, tm):
    m, d = h.shape
    const = lambda i: (0, 0)
    slab = lambda i: (layer, 0, 0)
    rows = pl.BlockSpec((tm, d), lambda i: (i, 0))
    resident = pl.Buffered(1)
    if last:
        out_specs, out_shape = [rows], [jax.ShapeDtypeStruct((m, d), F32)]
    else:
        out_specs = [rows, rows]
        out_shape = [jax.ShapeDtypeStruct((m, d), F32), jax.ShapeDtypeStruct((m, d), BF16)]
    return pl.pallas_call(
        functools.partial(_ple_kernel, last=last),
        grid=(m // tm,),
        in_specs=[rows, rows,
                  pl.BlockSpec((1, d), const),
                  pl.BlockSpec((1,) + w_gate.shape[1:], slab, pipeline_mode=resident),
                  pl.BlockSpec((1, tm, p.shape[2]), lambda i: (layer, i, 0)),
                  pl.BlockSpec((1,) + w_p.shape[1:], slab, pipeline_mode=resident),
                  pl.BlockSpec((1, d), const)],
        out_specs=out_specs,
        out_shape=out_shape,
        compiler_params=_params("parallel"),
        name="ple_gate",
    )(h, y, g, w_gate, p, w_p, g_next)


def _tile(n, target):
    t = min(n, target)
    assert n % t == 0, (n, t)
    return t


def _tiles(m, d, dff):
    return {
        "proj_m": _tile(m, 1024),
        "mix_m": _tile(m, 256),
        "ffn_m": _tile(m, 1024), "ffn_f": _tile(dff, 1024),
        "ple_m": _tile(m, 512),
    }


def kernel(x, p, w_in, conv_w, hg_lb_logits, hg_norm_g, att_rel_bias, w_branch, w_o,
           w_ff1, w_ff2, w_ple_in, w_ple_gate, g_mix, g_ff, g_ple, g_final):
    depth = w_in.shape[0]
    b, s, d = x.shape
    m = b * s
    w = MIX_WIDTH
    assert s % ATT_GQ == 0 and s % HG_BLOCK == 0 and d % w == 0
    assert w_in.shape[2] == (REF_FIRST_GATE_TILE + 3 * d // w) * w

    lb_sm = jax.nn.softmax(hg_lb_logits.astype(F32), axis=0)
    lb_all = jnp.cumsum(lb_sm, axis=0) - lb_sm[0]

    col_scale = jnp.ones((w_in.shape[2],), F32).at[REF_AQ_TILE * w:(REF_AQ_TILE + 1) * w].set(
        HEAD_DIM ** -0.5 * LOG2_E)
    w_in_b = (w_in * col_scale).astype(BF16)
    w_branch_b, w_o_b = w_branch.astype(BF16), w_o.astype(BF16)
    w_ff1_b, w_ff2_b = w_ff1.astype(BF16), w_ff2.astype(BF16)
    w_gate_b, w_ple_b = w_ple_gate.astype(BF16), w_ple_in.astype(BF16)
    p2 = p.reshape(depth, m, p.shape[3])

    t = _tiles(m, d, w_ff1.shape[2])
    h = x.reshape(m, d)
    xn = h
    for l in range(depth):
        last = l == depth - 1
        proj, z = in_proj(xn, g_mix[l].reshape(1, d), w_in_b, l, tm=t["proj_m"])
        proj3 = proj.reshape(b, s, proj.shape[1])
        o_hg = hgrn_branch(proj3, z.reshape(b, s, w), lb_all[l], HG_HEADS_PER_STEP)
        y_att = attention_branch(proj3, att_rel_bias[l])
        h1, hn = mix_out(proj, o_hg.reshape(m, w), y_att.reshape(m, w), h, conv_w[l], hg_norm_g[l],
                         w_branch_b, w_o_b, l, g_ff[l].reshape(1, d), seq_len=s, tm=t["mix_m"])
        y = ffn(hn, w_ff1_b, w_ff2_b, l, tm=t["ffn_m"], tf=t["ffn_f"])
        g_next = (g_final if last else g_mix[l + 1]).reshape(1, d)
        outs = ple_gate(h1, y, g_ple[l].reshape(1, d), w_gate_b, p2, w_ple_b, l, g_next, last, tm=t["ple_m"])
        if last:
            h, = outs
        else:
            h, xn = outs
    return h.reshape(b, s, d)
```
